```python
import jax, jax.numpy as jnp
from jax import lax
import numpy as np

D_MODEL = 1024
BATCH = 4
SEQ = 8192
DEPTH = 2
DEC_BATCH = 8
DEC_SEQ = 2048
PAST_LEN = 128

GRID_W = 64
NA_HEADS = 16
NA_HEAD_DIM = D_MODEL // NA_HEADS
NA_KR = 8
NA_KW = 16
POOL_WINDOWS = (2, 4, 8, 16)
POOL_GROUPS = len(POOL_WINDOWS)
POOL_GROUP_DIM = D_MODEL // POOL_GROUPS
D_FF = (D_MODEL * 7) // 2
N_EXPERTS = 8
TOP_K = 2
RMS_EPS = 1e-6

kernel_name = "hybrid_natten_pool_moe_encoder"


def _rmsnorm(x, g):
    xf = x.astype(jnp.float32)
    y = xf * lax.rsqrt(jnp.mean(xf * xf, axis=-1, keepdims=True) + RMS_EPS)
    return (y * g.astype(jnp.float32)).astype(x.dtype)


def _adaln(c, w, b):
    mod = jax.nn.silu(c) @ w + b
    shift, scale, gate = jnp.split(mod, 3, axis=-1)
    return shift[:, None, :], scale[:, None, :], gate[:, None, :]


def _neighborhood_attention(h, w_qkv, rpb, w_o):
    b, t, d = h.shape
    rows = t // GRID_W
    kr = min(NA_KR, rows)
    kw = NA_KW
    qkv = (h @ w_qkv).reshape(b, rows, GRID_W, 3, NA_HEADS, NA_HEAD_DIM)
    qkv = jnp.transpose(qkv, (3, 0, 4, 1, 2, 5))
    q = qkv[0] * (NA_HEAD_DIM ** -0.5)
    k = qkv[1]
    v = qkv[2]
    cols = jnp.arange(GRID_W, dtype=jnp.int32)
    col_start = jnp.clip(cols - kw // 2, 0, GRID_W - kw)
    col_idx = col_start[:, None] + jnp.arange(kw, dtype=jnp.int32)[None, :]
    dc_idx = col_idx - cols[:, None] + (NA_KW - 1)

    def one_row(r):
        rs = jnp.clip(r - kr // 2, 0, rows - kr)
        k_rows = lax.dynamic_slice_in_dim(k, rs, kr, axis=2)
        v_rows = lax.dynamic_slice_in_dim(v, rs, kr, axis=2)
        k_win = k_rows[:, :, :, col_idx, :]
        v_win = v_rows[:, :, :, col_idx, :]
        q_r = lax.dynamic_index_in_dim(q, r, axis=2, keepdims=False)
        s = jnp.einsum("bhqd,bhrqkd->bhqrk", q_r, k_win).astype(jnp.float32)
        dr_idx = rs + jnp.arange(kr, dtype=jnp.int32) - r + (NA_KR - 1)
        bias = rpb[:, dr_idx][:, :, dc_idx]
        s = s + jnp.transpose(bias, (0, 2, 1, 3))[None].astype(jnp.float32)
        p = jax.nn.softmax(s.reshape(b, NA_HEADS, GRID_W, kr * kw), axis=-1)
        p = p.reshape(b, NA_HEADS, GRID_W, kr, kw).astype(v.dtype)
        return jnp.einsum("bhqrk,bhrqkd->bhqd", p, v_win)

    o = lax.map(one_row, jnp.arange(rows, dtype=jnp.int32))
    o = jnp.transpose(o, (1, 0, 3, 2, 4)).reshape(b, t, d)
    return o @ w_o


def _pool_mixer(h, w_in, w_grp, scale, w_out):
    b, t, d = h.shape
    u = (h @ w_in).reshape(b, t, POOL_GROUPS, POOL_GROUP_DIM)
    uf = u.astype(jnp.float32)
    cs = jnp.concatenate([jnp.zeros((b, 1, POOL_GROUPS, POOL_GROUP_DIM), jnp.float32),
                          jnp.cumsum(uf, axis=1)], axis=1)
    half = jnp.array([w // 2 for w in POOL_WINDOWS], dtype=jnp.int32)
    pos = jnp.arange(t, dtype=jnp.int32)[:, None]
    lo = jnp.clip(pos - half[None, :], 0, t)
    hi = jnp.clip(pos + half[None, :], 0, t)
    gid = jnp.arange(POOL_GROUPS, dtype=jnp.int32)[None, :]
    win_sum = cs[:, hi, gid] - cs[:, lo, gid]
    cnt = (hi - lo).astype(jnp.float32)[None, :, :, None]
    pooled = (win_sum / cnt - uf).astype(h.dtype)
    y = jnp.einsum("btgc,gce->btge", pooled, w_grp) * scale.reshape(POOL_GROUPS, POOL_GROUP_DIM)
    return y.reshape(b, t, d) @ w_out


def _swiglu(h, w_gate, w_up, w_down):
    return (jax.nn.silu(h @ w_gate) * (h @ w_up)) @ w_down


def _moe_swiglu(h, w_router, b_router, w_gate, w_up, w_down):
    logits = (h @ w_router).astype(jnp.float32) + b_router.astype(jnp.float32)
    top_v, top_i = lax.top_k(logits, TOP_K)
    top_w = jax.nn.softmax(top_v, axis=-1)
    gates = jnp.sum(jax.nn.one_hot(top_i, N_EXPERTS, dtype=jnp.float32) * top_w[..., None],
                    axis=-2).astype(h.dtype)
    out = jnp.zeros_like(h)
    for e in range(N_EXPERTS):
        out = out + gates[..., e:e + 1] * _swiglu(h, w_gate[e], w_up[e], w_down[e])
    return out


def _trunk(x, c, ln_mix_g, ada_mix_w, ada_mix_b, ln_ffn_g, ada_ffn_w, ada_ffn_b,
           na_w_qkv, na_rpb, na_w_o, pool_w_in, pool_w_grp, pool_scale, pool_w_out,
           ffn_w_gate, ffn_w_up, ffn_w_down,
           moe_w_router, moe_b_router, moe_w_gate, moe_w_up, moe_w_down, ln_f_g):
    for i in range(DEPTH):
        j = i // 2
        shift, scale, gate = _adaln(c, ada_mix_w[i], ada_mix_b[i])
        h = _rmsnorm(x, ln_mix_g[i]) * (1 + scale) + shift
        if i % 2 == 0:
            m = _neighborhood_attention(h, na_w_qkv[j], na_rpb[j], na_w_o[j])
        else:
            m = _pool_mixer(h, pool_w_in[j], pool_w_grp[j], pool_scale[j], pool_w_out[j])
        x = x + gate * m
        shift, scale, gate = _adaln(c, ada_ffn_w[i], ada_ffn_b[i])
        h = _rmsnorm(x, ln_ffn_g[i]) * (1 + scale) + shift
        if i % 2 == 0:
            f = _swiglu(h, ffn_w_gate[j], ffn_w_up[j], ffn_w_down[j])
        else:
            f = _moe_swiglu(h, moe_w_router[j], moe_b_router[j], moe_w_gate[j], moe_w_up[j], moe_w_down[j])
        x = x + gate * f
    return _rmsnorm(x, ln_f_g)


def setup_inputs(seed: int = 0) -> dict:
    key = jax.random.key(seed)
    ks = jax.random.split(key, 26)
    d, f, e = D_MODEL, D_FF, N_EXPERTS
    n_a = (DEPTH + 1) // 2
    n_b = DEPTH // 2
    c_dim = POOL_GROUP_DIM

    def nrm(k, shape, scale):
        return jax.random.normal(k, shape, jnp.float32) * scale

    return {
        "x_prompt": nrm(ks[0], (BATCH, SEQ, d), 1.0),
        "x_sample": nrm(ks[1], (DEC_BATCH, DEC_SEQ, d), 1.0),
        "c_prompt": nrm(ks[2], (BATCH, d), 1.0),
        "c_sample": nrm(ks[3], (DEC_BATCH, d), 1.0),
        "ln_mix_g": 1.0 + nrm(ks[4], (DEPTH, d), 0.1),
        "ada_mix_w": nrm(ks[5], (DEPTH, d, 3 * d), 0.5 * d ** -0.5),
        "ada_mix_b": nrm(ks[6], (DEPTH, 3 * d), 0.02),
        "ln_ffn_g": 1.0 + nrm(ks[7], (DEPTH, d), 0.1),
        "ada_ffn_w": nrm(ks[8], (DEPTH, d, 3 * d), 0.5 * d ** -0.5),
        "ada_ffn_b": nrm(ks[9], (DEPTH, 3 * d), 0.02),
        "na_w_qkv": nrm(ks[10], (n_a, d, 3 * d), d ** -0.5),
        "na_rpb": nrm(ks[11], (n_a, NA_HEADS, 2 * NA_KR - 1, 2 * NA_KW - 1), 0.5),
        "na_w_o": nrm(ks[12], (n_a, d, d), d ** -0.5),
        "pool_w_in": nrm(ks[13], (n_b, d, d), d ** -0.5),
        "pool_w_grp": nrm(ks[14], (n_b, POOL_GROUPS, c_dim, c_dim), c_dim ** -0.5),
        "pool_scale": 1.0 + nrm(ks[15], (n_b, d), 0.1),
        "pool_w_out": nrm(ks[16], (n_b, d, d), d ** -0.5),
        "ffn_w_gate": nrm(ks[17], (n_a, d, f), d ** -0.5),
        "ffn_w_up": nrm(ks[18], (n_a, d, f), d ** -0.5),
        "ffn_w_down": nrm(ks[19], (n_a, f, d), f ** -0.5),
        "moe_w_router": nrm(ks[20], (n_b, d, e), d ** -0.5),
        "moe_b_router": nrm(ks[21], (n_b, e), 0.01),
        "moe_w_gate": nrm(ks[22], (n_b, e, d, f), d ** -0.5),
        "moe_w_up": nrm(ks[23], (n_b, e, d, f), d ** -0.5),
        "moe_w_down": nrm(ks[24], (n_b, e, f, d), f ** -0.5),
        "ln_f_g": 1.0 + nrm(ks[25], (d,), 0.1),
    }


def reference(x_prompt, x_sample, c_prompt, c_sample,
              ln_mix_g, ada_mix_w, ada_mix_b, ln_ffn_g, ada_ffn_w, ada_ffn_b,
              na_w_qkv, na_rpb, na_w_o, pool_w_in, pool_w_grp, pool_scale, pool_w_out,
              ffn_w_gate, ffn_w_up, ffn_w_down,
              moe_w_router, moe_b_router, moe_w_gate, moe_w_up, moe_w_down, ln_f_g):
    y_prompt = _trunk(x_prompt, c_prompt, ln_mix_g, ada_mix_w, ada_mix_b, ln_ffn_g, ada_ffn_w, ada_ffn_b,
                      na_w_qkv, na_rpb, na_w_o, pool_w_in, pool_w_grp, pool_scale, pool_w_out,
                      ffn_w_gate, ffn_w_up, ffn_w_down,
                      moe_w_router, moe_b_router, moe_w_gate, moe_w_up, moe_w_down, ln_f_g)
    y_sample = _trunk(x_sample, c_sample, ln_mix_g, ada_mix_w, ada_mix_b, ln_ffn_g, ada_ffn_w, ada_ffn_b,
                      na_w_qkv, na_rpb, na_w_o, pool_w_in, pool_w_grp, pool_scale, pool_w_out,
                      ffn_w_gate, ffn_w_up, ffn_w_down,
                      moe_w_router, moe_b_router, moe_w_gate, moe_w_up, moe_w_down, ln_f_g)
    return (y_prompt, y_sample)
```

```python
import functools

import jax
import jax.numpy as jnp
from jax import lax
from jax.experimental import pallas as pl
from jax.experimental.pallas import tpu as pltpu

F32 = jnp.float32
BF16 = jnp.bfloat16

RMS_EPS = 1e-6
GRID_W = 64
NA_HEADS = 16
HEAD_DIM = 64
NA_KR = 8
NA_KW = 16
POOL_HALF = (1, 2, 4, 8)
N_EXPERTS = 8
LANES = 128
SUBLANES = 8
MASK_BIAS = -1e30
ROW_PAIR = 2 * GRID_W
WIN_ROWS = NA_KR + 2
BIAS_TILES = 2 * NA_KR
VMEM_LIMIT = 48 * 1024 * 1024


def _silu(a):
    return a * jax.nn.sigmoid(a)


def _mod_rmsnorm(x, g, shift, scale):
    var = jnp.mean(x * x, axis=-1, keepdims=True)
    y = (x * lax.rsqrt(var + RMS_EPS)) * g
    return y * (1.0 + scale) + shift


def _params(*sem, vmem=VMEM_LIMIT):
    return pltpu.CompilerParams(dimension_semantics=sem, vmem_limit_bytes=vmem)


def _adaln_kernel(c_ref, w_ref, b_ref, o_ref):
    s = _silu(c_ref[...]).astype(BF16)
    o_ref[0] = jnp.dot(s, w_ref[0].astype(BF16), preferred_element_type=F32) + b_ref[0]


def _adaln(c_all, w, b):
    depth, d, d3 = w.shape
    r = c_all.shape[0]
    tn = 1024
    return pl.pallas_call(
        _adaln_kernel,
        grid=(depth, d3 // tn),
        in_specs=[
            pl.BlockSpec((r, d), lambda i, j: (0, 0)),
            pl.BlockSpec((1, d, tn), lambda i, j: (i, 0, j)),
            pl.BlockSpec((1, 1, tn), lambda i, j: (i, 0, j)),
        ],
        out_specs=pl.BlockSpec((1, r, tn), lambda i, j: (i, 0, j)),
        out_shape=jax.ShapeDtypeStruct((depth, r, d3), F32),
        compiler_params=_params("parallel", "parallel"),
        name="adaln",
    )(c_all, w, b.reshape(depth, 1, d3))


def _qkv_kernel(x_ref, mod_ref, g_ref, wq_ref, wk_ref, wvt_ref, q_ref, k_ref, vt_ref):
    d = x_ref.shape[-1]
    h = _mod_rmsnorm(x_ref[0], g_ref[...], mod_ref[0, :, 0:d], mod_ref[0, :, d:2 * d]).astype(BF16)
    q = jnp.dot(h, wq_ref[...], preferred_element_type=F32) * (HEAD_DIM ** -0.5)
    q_ref[0] = q.astype(BF16)
    k_ref[0] = jnp.dot(h, wk_ref[...], preferred_element_type=F32).astype(BF16)
    vt = lax.dot_general(wvt_ref[...], h, (((1,), (1,)), ((), ())), preferred_element_type=F32)
    for j in range(vt_ref.shape[1]):
        vt_ref[0, j] = vt[:, j * LANES:(j + 1) * LANES].astype(BF16)


def _qkv(x, mod, g, wq, wk, wvt, tm):
    b, t, d = x.shape
    const = lambda i, j: (0, 0)
    return pl.pallas_call(
        _qkv_kernel,
        grid=(b, t // tm),
        in_specs=[
            pl.BlockSpec((1, tm, d), lambda i, j: (i, j, 0)),
            pl.BlockSpec((1, 1, 3 * d), lambda i, j: (i, 0, 0)),
            pl.BlockSpec((1, d), const),
            pl.BlockSpec((d, d), const),
            pl.BlockSpec((d, d), const),
            pl.BlockSpec((d, d), const),
        ],
        out_specs=[
            pl.BlockSpec((1, tm, d), lambda i, j: (i, j, 0)),
            pl.BlockSpec((1, tm, d), lambda i, j: (i, j, 0)),
            pl.BlockSpec((1, tm // LANES, d, LANES), lambda i, j: (i, j, 0, 0)),
        ],
        out_shape=[
            jax.ShapeDtypeStruct((b, t, d), BF16),
            jax.ShapeDtypeStruct((b, t, d), BF16),
            jax.ShapeDtypeStruct((b, t // LANES, d, LANES), BF16),
        ],
        compiler_params=_params("parallel", "parallel"),
        name="qkv",
    )(x, mod, g, wq, wk, wvt)


def _na_bias_table(rpb):
    h = rpb.shape[0]
    c = jnp.arange(GRID_W, dtype=jnp.int32)
    cs = jnp.clip(c - NA_KW // 2, 0, GRID_W - NA_KW)
    kc = c[:, None]
    valid = (kc >= cs[None, :]) & (kc < cs[None, :] + NA_KW)
    dc = jnp.clip(kc - c[None, :] + (NA_KW - 1), 0, 2 * NA_KW - 2)
    bt = jnp.where(valid, rpb.astype(F32)[:, :, dc], MASK_BIAS)
    bt = jnp.concatenate([bt, jnp.full((h, 1, GRID_W, GRID_W), MASK_BIAS, F32)], axis=1)
    bt = bt.reshape(h // 2, 2, BIAS_TILES, GRID_W, GRID_W).transpose(0, 2, 3, 1, 4)
    return bt.reshape(h // 2, BIAS_TILES, GRID_W, 2 * GRID_W)


def _na_kernel(q_ref, k_ref, vt_ref, bias_ref, o_ref, *, rows, pairs_per_step, row_pairs_per_step):
    step = pl.program_id(2)
    lane = lax.broadcasted_iota(jnp.int32, (ROW_PAIR, LANES), 1)
    first_head = lane < HEAD_DIM
    first_head_row = lax.broadcasted_iota(jnp.int32, (GRID_W, LANES), 1) < HEAD_DIM

    def row_pair(j, carry):
        rp = step * row_pairs_per_step + j
        r0 = 2 * rp
        wblk = jnp.clip(rp - NA_KR // 4, 0, (rows - WIN_ROWS) // 2)
        ws = 2 * wblk
        tok = pl.multiple_of(j * ROW_PAIR, ROW_PAIR)
        ktok = pl.multiple_of(ws * GRID_W, ROW_PAIR)
        for p in range(pairs_per_step):
            cols = slice(p * LANES, (p + 1) * LANES)
            q2 = q_ref[0, pl.ds(tok, ROW_PAIR), cols]
            qa = jnp.where(first_head, q2, jnp.zeros_like(q2))
            qb = jnp.where(first_head, jnp.zeros_like(q2), q2)
            qbd = jnp.concatenate([qa[:GRID_W], qb[:GRID_W], qa[GRID_W:], qb[GRID_W:]], axis=0)
            kwin = k_ref[0, pl.ds(ktok, WIN_ROWS * GRID_W), cols]
            st = lax.dot_general(kwin, qbd, (((1,), (1,)), ((), ())), preferred_element_type=F32)
            blocks = []
            for i in range(WIN_ROWS):
                kr = ws + i
                tiles = []
                for s in range(2):
                    r = r0 + s
                    rs = jnp.clip(r - NA_KR // 2, 0, rows - NA_KR)
                    inside = jnp.logical_and(kr >= rs, kr < rs + NA_KR)
                    idx = jnp.where(inside, kr - r + (NA_KR - 1), BIAS_TILES - 1)
                    tiles.append(bias_ref[p, idx])
                blocks.append(st[i * GRID_W:(i + 1) * GRID_W] + jnp.concatenate(tiles, axis=1))
            sc = jnp.concatenate(blocks, axis=0)
            m = jnp.max(sc, axis=0, keepdims=True)
            e = jnp.exp(sc - m)
            l = jnp.sum(e, axis=0, keepdims=True)
            pt = e.astype(BF16)
            ot = jnp.zeros((LANES, 2 * LANES), F32)
            for c in range(WIN_ROWS * GRID_W // LANES):
                vt = vt_ref[0, wblk + c, cols, :]
                ot = ot + jnp.dot(vt, pt[c * LANES:(c + 1) * LANES], preferred_element_type=F32)
            ot = ot / l
            outs = []
            for s in range(2):
                tt = ot[:, s * LANES:(s + 1) * LANES].T
                outs.append(jnp.where(first_head_row, tt[:GRID_W], tt[GRID_W:]))
            o_ref[0, pl.ds(tok, ROW_PAIR), cols] = jnp.concatenate(outs, axis=0).astype(BF16)
        return carry

    lax.fori_loop(0, row_pairs_per_step, row_pair, 0)


def _na(q, k, vt, bias, *, head_groups, row_pairs_per_step):
    b, t, d = q.shape
    rows = t // GRID_W
    assert rows % 2 == 0 and rows >= WIN_ROWS and (rows // 2) % row_pairs_per_step == 0
    gc = d // head_groups
    pairs = gc // LANES
    tq = row_pairs_per_step * ROW_PAIR
    kern = functools.partial(_na_kernel, rows=rows, pairs_per_step=pairs, row_pairs_per_step=row_pairs_per_step)
    return pl.pallas_call(
        kern,
        grid=(b, head_groups, t // tq),
        in_specs=[
            pl.BlockSpec((1, tq, gc), lambda i, g, j: (i, j, g)),
            pl.BlockSpec((1, t, gc), lambda i, g, j: (i, 0, g)),
            pl.BlockSpec((1, t // LANES, gc, LANES), lambda i, g, j: (i, 0, g, 0)),
            pl.BlockSpec((pairs, BIAS_TILES, GRID_W, LANES), lambda i, g, j: (g, 0, 0, 0)),
        ],
        out_specs=pl.BlockSpec((1, tq, gc), lambda i, g, j: (i, j, g)),
        out_shape=jax.ShapeDtypeStruct((b, t, d), BF16),
        compiler_params=_params("parallel", "parallel", "parallel"),
        name="na_attn",
    )(q, k, vt, bias)


def _proj_ffn_kernel(x_ref, o_ref, mm_ref, mf_ref, g_ref, wo_ref, wg_ref, wu_ref, wd_ref, out_ref, x1_s, h_s, acc_s):
    d = x_ref.shape[-1]
    f = pl.program_id(2)

    @pl.when(f == 0)
    def _():
        m = jnp.dot(o_ref[0], wo_ref[...], preferred_element_type=F32)
        x1 = x_ref[0] + mm_ref[0, :, 2 * d:3 * d] * m
        x1_s[...] = x1
        h_s[...] = _mod_rmsnorm(x1, g_ref[...], mf_ref[0, :, 0:d], mf_ref[0, :, d:2 * d]).astype(BF16)
        acc_s[...] = jnp.zeros_like(acc_s)

    h = h_s[...]
    a = jnp.dot(h, wg_ref[...], preferred_element_type=F32)
    u = jnp.dot(h, wu_ref[...], preferred_element_type=F32)
    acc_s[...] += jnp.dot((_silu(a) * u).astype(BF16), wd_ref[...], preferred_element_type=F32)

    @pl.when(f == pl.num_programs(2) - 1)
    def _():
        out_ref[0] = x1_s[...] + mf_ref[0, :, 2 * d:3 * d] * acc_s[...]


def _proj_ffn(x, o, mod_mix, mod_ffn, g, wo, wg, wu, wd, tm, tf):
    b, t, d = x.shape
    ff = wg.shape[1]
    tok = lambda i, j, f: (i, j, 0)
    seq = lambda i, j, f: (i, 0, 0)
    return pl.pallas_call(
        _proj_ffn_kernel,
        grid=(b, t // tm, ff // tf),
        in_specs=[
            pl.BlockSpec((1, tm, d), tok),
            pl.BlockSpec((1, tm, d), tok),
            pl.BlockSpec((1, 1, 3 * d), seq),
            pl.BlockSpec((1, 1, 3 * d), seq),
            pl.BlockSpec((1, d), lambda i, j, f: (0, 0)),
            pl.BlockSpec((d, d), lambda i, j, f: (0, 0)),
            pl.BlockSpec((d, tf), lambda i, j, f: (0, f)),
            pl.BlockSpec((d, tf), lambda i, j, f: (0, f)),
            pl.BlockSpec((tf, d), lambda i, j, f: (f, 0)),
        ],
        out_specs=pl.BlockSpec((1, tm, d), tok),
        out_shape=jax.ShapeDtypeStruct((b, t, d), F32),
        scratch_shapes=[pltpu.VMEM((tm, d), F32), pltpu.VMEM((tm, d), BF16), pltpu.VMEM((tm, d), F32)],
        compiler_params=_params("parallel", "parallel", "arbitrary"),
        name="proj_ffn",
    )(x, o, mod_mix, mod_ffn, g, wo, wg, wu, wd)


def _pool_kernel(x_ref, xp_ref, xn_ref, mod_ref, g_ref, win_ref, wgrp_ref, ps_ref, wout_ref, out_ref, *, seq_len):
    d = x_ref.shape[-1]
    tm = x_ref.shape[1]
    halo = SUBLANES
    n = tm + 2 * halo
    j = pl.program_id(1)
    x = x_ref[0]
    xe = jnp.concatenate([xp_ref[0], x, xn_ref[0]], axis=0)
    h = _mod_rmsnorm(xe, g_ref[...], mod_ref[0, :, 0:d], mod_ref[0, :, d:2 * d]).astype(BF16)
    u = jnp.dot(h, win_ref[...], preferred_element_type=F32)
    pos = j * tm - halo + lax.broadcasted_iota(jnp.int32, (n, 1), 0)
    u = jnp.where(jnp.logical_and(pos >= 0, pos < seq_len), u, 0.0)
    gc = d // len(POOL_HALF)
    ys = []
    for gi, half in enumerate(POOL_HALF):
        ug = u[:, gi * gc:(gi + 1) * gc]
        win = ug + pltpu.roll(ug, 1, axis=0)
        span = 1
        while span < half:
            win = pltpu.roll(win, span, axis=0) + pltpu.roll(win, n - span, axis=0)
            span *= 2
        cnt = jnp.clip(pos + half, 0, seq_len) - jnp.clip(pos - half, 0, seq_len)
        cnt = jnp.maximum(cnt, 1).astype(F32)
        pooled = (win / cnt - ug)[halo:halo + tm].astype(BF16)
        y = jnp.dot(pooled, wgrp_ref[gi], preferred_element_type=F32) * ps_ref[:, gi * gc:(gi + 1) * gc]
        ys.append(y.astype(BF16))
    m = jnp.dot(jnp.concatenate(ys, axis=1), wout_ref[...], preferred_element_type=F32)
    out_ref[0] = x + mod_ref[0, :, 2 * d:3 * d] * m


def _pool(x, mod, g, w_in, w_grp, pscale, w_out, tm):
    b, t, d = x.shape
    hb = tm // SUBLANES
    last = t // SUBLANES - 1
    const2 = lambda i, j: (0, 0)
    return pl.pallas_call(
        functools.partial(_pool_kernel, seq_len=t),
        grid=(b, t // tm),
        in_specs=[
            pl.BlockSpec((1, tm, d), lambda i, j: (i, j, 0)),
            pl.BlockSpec((1, SUBLANES, d), lambda i, j: (i, jnp.maximum(j * hb - 1, 0), 0)),
            pl.BlockSpec((1, SUBLANES, d), lambda i, j: (i, jnp.minimum((j + 1) * hb, last), 0)),
            pl.BlockSpec((1, 1, 3 * d), lambda i, j: (i, 0, 0)),
            pl.BlockSpec((1, d), const2),
            pl.BlockSpec((d, d), const2),
            pl.BlockSpec(w_grp.shape, lambda i, j: (0, 0, 0)),
            pl.BlockSpec((1, d), const2),
            pl.BlockSpec((d, d), const2),
        ],
        out_specs=pl.BlockSpec((1, tm, d), lambda i, j: (i, j, 0)),
        out_shape=jax.ShapeDtypeStruct((b, t, d), F32),
        compiler_params=_params("parallel", "parallel"),
        name="pool_mixer",
    )(x, x, x, mod, g, w_in, w_grp, pscale, w_out)


def _router_kernel(x_ref, mod_ref, g_ref, wr_ref, br_ref, h_ref, w_ref, sel_ref, cnt_ref, carry_s):
    d = x_ref.shape[-1]
    tm = x_ref.shape[1]

    @pl.when(jnp.logical_and(pl.program_id(0) == 0, pl.program_id(1) == 0))
    def _():
        carry_s[...] = jnp.zeros_like(carry_s)

    h = _mod_rmsnorm(x_ref[0], g_ref[...], mod_ref[0, :, 0:d], mod_ref[0, :, d:2 * d])
    h_ref[0] = h
    logits = jnp.dot(h.astype(BF16), wr_ref[...], preferred_element_type=F32) + br_ref[...]
    lane = lax.broadcasted_iota(jnp.int32, logits.shape, 1).astype(F32)
    v1 = jnp.max(logits, axis=-1, keepdims=True)
    i1 = jnp.min(jnp.where(logits == v1, lane, float(LANES)), axis=-1, keepdims=True)
    rest = jnp.where(lane == i1, -jnp.inf, logits)
    v2 = jnp.max(rest, axis=-1, keepdims=True)
    i2 = jnp.min(jnp.where(rest == v2, lane, float(LANES)), axis=-1, keepdims=True)
    e2 = jnp.exp(v2 - v1)
    w1 = 1.0 / (1.0 + e2)
    w2 = e2 / (1.0 + e2)
    onehot = jnp.logical_or(lane == i1, lane == i2)
    row = lax.broadcasted_iota(jnp.int32, (tm, tm), 0)
    col = lax.broadcasted_iota(jnp.int32, (tm, tm), 1)
    before = jnp.where(col < row, 1.0, 0.0).astype(BF16)
    sel = jnp.where(onehot, 1.0, 0.0)
    rank = carry_s[...] + jnp.dot(before, sel.astype(BF16), preferred_element_type=F32)
    carry = carry_s[...] + jnp.sum(sel, axis=0, keepdims=True)
    carry_s[...] = carry
    cnt_ref[...] = jnp.broadcast_to(carry, cnt_ref.shape).astype(jnp.int32)
    r1 = jnp.sum(jnp.where(lane == i1, rank, 0.0), axis=-1, keepdims=True)
    r2 = jnp.sum(jnp.where(lane == i2, rank, 0.0), axis=-1, keepdims=True)
    w_ref[0] = jnp.where(lane == 0, w1, jnp.where(lane == 1, w2, 0.0))
    info = jnp.where(lane == 0, i1, jnp.where(lane == 1, r1, jnp.where(lane == 2, i2, jnp.where(lane == 3, r2, 0.0))))
    sel_ref[0] = info.astype(jnp.int32)


def _router(x, mod, g, wr, br, tm):
    b, t, d = x.shape
    const2 = lambda i, j: (0, 0)
    tok = lambda i, j: (i, j, 0)
    return pl.pallas_call(
        _router_kernel,
        grid=(b, t // tm),
        in_specs=[
            pl.BlockSpec((1, tm, d), tok),
            pl.BlockSpec((1, 1, 3 * d), lambda i, j: (i, 0, 0)),
            pl.BlockSpec((1, d), const2),
            pl.BlockSpec((d, LANES), const2),
            pl.BlockSpec((1, LANES), const2),
        ],
        out_specs=[
            pl.BlockSpec((1, tm, d), tok),
            pl.BlockSpec((1, tm, LANES), tok),
            pl.BlockSpec((1, tm, LANES), tok),
            pl.BlockSpec((SUBLANES, LANES), const2),
        ],
        out_shape=[
            jax.ShapeDtypeStruct((b, t, d), F32),
            jax.ShapeDtypeStruct((b, t, LANES), F32),
            jax.ShapeDtypeStruct((b, t, LANES), jnp.int32),
            jax.ShapeDtypeStruct((SUBLANES, LANES), jnp.int32),
        ],
        scratch_shapes=[pltpu.VMEM((1, LANES), F32)],
        compiler_params=_params("arbitrary", "arbitrary"),
        name="router",
    )(x, mod, g, wr, br)


def _dispatch_kernel(p1_ref, p2_ref, h_ref, xs_in_ref, xs_ref, sem):
    del xs_in_ref
    tm = h_ref.shape[0]

    def row_copy(i, pos):
        return pltpu.make_async_copy(h_ref.at[pl.ds(i, 1), :], xs_ref.at[pl.ds(pos, 1), :], sem)

    def issue(i, carry):
        row_copy(i, p1_ref[0, 0, i]).start()
        row_copy(i, p2_ref[0, 0, i]).start()
        return carry

    lax.fori_loop(0, tm, issue, 0)

    def drain(i, carry):
        row_copy(0, 0).wait()
        row_copy(0, 0).wait()
        return carry

    lax.fori_loop(0, tm, drain, 0)


def _dispatch(h, pos1, pos2, slots, tm):
    n, d = h.shape
    smem = lambda: pl.BlockSpec((1, 1, tm), lambda i: (i, 0, 0), memory_space=pltpu.SMEM)
    return pl.pallas_call(
        _dispatch_kernel,
        grid=(n // tm,),
        in_specs=[
            smem(),
            smem(),
            pl.BlockSpec((tm, d), lambda i: (i, 0)),
            pl.BlockSpec(memory_space=pl.ANY),
        ],
        out_specs=pl.BlockSpec(memory_space=pl.ANY),
        out_shape=jax.ShapeDtypeStruct((slots, d), F32),
        scratch_shapes=[pltpu.SemaphoreType.DMA(())],
        input_output_aliases={3: 0},
        compiler_params=_params("arbitrary"),
        name="moe_dispatch",
    )(pos1.reshape(n // tm, 1, tm), pos2.reshape(n // tm, 1, tm), h, jnp.zeros((slots, d), F32))


def _experts_kernel(te_ref, tv_ref, xs_ref, wg_ref, wu_ref, wd_ref, ys_ref, h_s, acc_s):
    i = pl.program_id(0)
    f = pl.program_id(1)
    last = pl.num_programs(1) - 1
    live = tv_ref[i] > 0

    @pl.when(jnp.logical_and(live, f == 0))
    def _():
        h_s[...] = xs_ref[...].astype(BF16)
        acc_s[...] = jnp.zeros_like(acc_s)

    @pl.when(live)
    def _():
        h = h_s[...]
        a = jnp.dot(h, wg_ref[0], preferred_element_type=F32)
        u = jnp.dot(h, wu_ref[0], preferred_element_type=F32)
        acc_s[...] += jnp.dot((_silu(a) * u).astype(BF16), wd_ref[0], preferred_element_type=F32)

    @pl.when(jnp.logical_and(live, f == last))
    def _():
        ys_ref[...] = acc_s[...]

    @pl.when(jnp.logical_and(jnp.logical_not(live), f == last))
    def _():
        ys_ref[...] = jnp.zeros_like(ys_ref)


def _experts(xs, tile_expert, tile_live, wg, wu, wd, tm, tf):
    slots, d = xs.shape
    ff = wg.shape[2]
    nf = ff // tf
    fidx = lambda f, tv, i: f * tv[i] + (nf - 1) * (1 - tv[i])
    grid_spec = pltpu.PrefetchScalarGridSpec(
        num_scalar_prefetch=2,
        grid=(slots // tm, nf),
        in_specs=[
            pl.BlockSpec((tm, d), lambda i, f, te, tv: (i, 0)),
            pl.BlockSpec((1, d, tf), lambda i, f, te, tv: (te[i], 0, fidx(f, tv, i))),
            pl.BlockSpec((1, d, tf), lambda i, f, te, tv: (te[i], 0, fidx(f, tv, i))),
            pl.BlockSpec((1, tf, d), lambda i, f, te, tv: (te[i], fidx(f, tv, i), 0)),
        ],
        out_specs=pl.BlockSpec((tm, d), lambda i, f, te, tv: (i, 0)),
        scratch_shapes=[pltpu.VMEM((tm, d), BF16), pltpu.VMEM((tm, d), F32)],
    )
    return pl.pallas_call(
        _experts_kernel,
        grid_spec=grid_spec,
        out_shape=jax.ShapeDtypeStruct((slots, d), F32),
        compiler_params=_params("parallel", "arbitrary"),
        name="moe_experts",
    )(tile_expert, tile_live, xs, wg, wu, wd)


def _combine_kernel(p1_ref, p2_ref, x_ref, w_ref, mod_ref, gf_ref, ys_ref, out_ref, y1_s, y2_s, sem):
    d = x_ref.shape[-1]
    tm = x_ref.shape[1]

    def row_copy(pos, dst, i):
        return pltpu.make_async_copy(ys_ref.at[pl.ds(pos, 1), :], dst.at[pl.ds(i, 1), :], sem)

    def issue(i, carry):
        row_copy(p1_ref[0, 0, i], y1_s, i).start()
        row_copy(p2_ref[0, 0, i], y2_s, i).start()
        return carry

    lax.fori_loop(0, tm, issue, 0)

    def drain(i, carry):
        row_copy(0, y1_s, 0).wait()
        row_copy(0, y2_s, 0).wait()
        return carry

    lax.fori_loop(0, tm, drain, 0)

    w = w_ref[0]
    f = w[:, 0:1] * y1_s[...] + w[:, 1:2] * y2_s[...]
    x = x_ref[0] + mod_ref[0, :, 2 * d:3 * d] * f
    var = jnp.mean(x * x, axis=-1, keepdims=True)
    out_ref[0] = (x * lax.rsqrt(var + RMS_EPS)) * gf_ref[...]


def _combine(x, w, mod, gf, ys, pos1, pos2, tm):
    b, t, d = x.shape
    nt = t // tm
    smem = lambda: pl.BlockSpec((1, 1, tm), lambda i, j: (i * nt + j, 0, 0), memory_space=pltpu.SMEM)
    tok = lambda i, j: (i, j, 0)
    return pl.pallas_call(
        _combine_kernel,
        grid=(b, nt),
        in_specs=[
            smem(),
            smem(),
            pl.BlockSpec((1, tm, d), tok),
            pl.BlockSpec((1, tm, LANES), tok),
            pl.BlockSpec((1, 1, 3 * d), lambda i, j: (i, 0, 0)),
            pl.BlockSpec((1, d), lambda i, j: (0, 0)),
            pl.BlockSpec(memory_space=pl.ANY),
        ],
        out_specs=pl.BlockSpec((1, tm, d), tok),
        out_shape=jax.ShapeDtypeStruct((b, t, d), F32),
        scratch_shapes=[pltpu.VMEM((tm, d), F32), pltpu.VMEM((tm, d), F32), pltpu.SemaphoreType.DMA(())],
        compiler_params=_params("arbitrary", "arbitrary"),
        name="moe_combine",
    )(pos1.reshape(b * nt, 1, tm), pos2.reshape(b * nt, 1, tm), x, w, mod, gf, ys)


def _tiles(t):
    tm = min(512, t)
    assert t % tm == 0 and tm % LANES == 0
    return tm


def _trunk(x, mods_mix, mods_ffn, p):
    b, t, d = x.shape
    tm = _tiles(t)
    tf = p["tf"]

    q, k, vt = _qkv(x, mods_mix[0], p["ln_mix_g"][0], p["wq"], p["wk"], p["wvt"], tm)
    rows = t // GRID_W
    rpp = 4 if (rows // 2) % 4 == 0 else 1
    o = _na(q, k, vt, p["na_bias"], head_groups=2, row_pairs_per_step=rpp)
    x = _proj_ffn(x, o, mods_mix[0], mods_ffn[0], p["ln_ffn_g"][0], p["wo"], p["ffn_wg"], p["ffn_wu"], p["ffn_wd"], tm, tf)

    x = _pool(x, mods_mix[1], p["ln_mix_g"][1], p["pool_w_in"], p["pool_w_grp"], p["pool_scale"], p["pool_w_out"], tm)
    h, w, sel, counts = _router(x, mods_ffn[1], p["ln_ffn_g"][1], p["wr"], p["br"], tm)

    n = b * t
    sel = sel.reshape(n, LANES)
    counts = counts[0, :N_EXPERTS]
    group = ((counts + tm - 1) // tm) * tm
    ends = jnp.cumsum(group)
    starts = ends - group
    pos1 = starts[sel[:, 0]] + sel[:, 1]
    pos2 = starts[sel[:, 2]] + sel[:, 3]
    n_tiles = (2 * n) // tm + N_EXPERTS
    tile_start = jnp.arange(n_tiles, dtype=jnp.int32) * tm
    tile_live = (tile_start < ends[-1]).astype(jnp.int32)
    last_expert = jnp.searchsorted(ends, ends[-1] - 1, side="right").astype(jnp.int32)
    tile_expert = jnp.where(tile_live > 0, jnp.searchsorted(ends, tile_start, side="right").astype(jnp.int32), last_expert)
    tile_expert = jnp.minimum(tile_expert, N_EXPERTS - 1)

    xs = _dispatch(h.reshape(n, d), pos1, pos2, n_tiles * tm, tm)
    ys = _experts(xs, tile_expert, tile_live, p["moe_wg"], p["moe_wu"], p["moe_wd"], tm, tf)
    return _combine(x, w, mods_ffn[1], p["ln_f_g"], ys, pos1, pos2, tm)


def kernel(x_prompt, x_sample, c_prompt, c_sample, ln_mix_g, ada_mix_w, ada_mix_b, ln_ffn_g, ada_ffn_w, ada_ffn_b, na_w_qkv, na_rpb, na_w_o, pool_w_in, pool_w_grp, pool_scale, pool_w_out, ffn_w_gate, ffn_w_up, ffn_w_down, moe_w_router, moe_b_router, moe_w_gate, moe_w_up, moe_w_down, ln_f_g):
    d = x_prompt.shape[-1]
    depth = ln_mix_g.shape[0]
    assert depth == 2 and d == NA_HEADS * HEAD_DIM
    bp, bs = c_prompt.shape[0], c_sample.shape[0]

    pad = (-(bp + bs)) % SUBLANES
    c_all = jnp.concatenate([c_prompt, c_sample, jnp.zeros((pad, d), F32)], axis=0)
    mods_mix = _adaln(c_all, ada_mix_w, ada_mix_b)
    mods_ffn = _adaln(c_all, ada_ffn_w, ada_ffn_b)

    wqkv = na_w_qkv[0]
    ff = ffn_w_gate.shape[-1]
    tf = 512 if ff % 512 == 0 else ff
    ne = moe_w_router.shape[-1]
    assert ne == N_EXPERTS
    p = {
        "tf": tf,
        "ln_mix_g": ln_mix_g.reshape(depth, 1, d),
        "ln_ffn_g": ln_ffn_g.reshape(depth, 1, d),
        "ln_f_g": ln_f_g.reshape(1, d),
        "wq": wqkv[:, 0:d].astype(BF16),
        "wk": wqkv[:, d:2 * d].astype(BF16),
        "wvt": wqkv[:, 2 * d:3 * d].T.astype(BF16),
        "na_bias": _na_bias_table(na_rpb[0]),
        "wo": na_w_o[0].astype(BF16),
        "ffn_wg": ffn_w_gate[0].astype(BF16),
        "ffn_wu": ffn_w_up[0].astype(BF16),
        "ffn_wd": ffn_w_down[0].astype(BF16),
        "pool_w_in": pool_w_in[0].astype(BF16),
        "pool_w_grp": pool_w_grp[0].astype(BF16),
        "pool_scale": pool_scale[0].reshape(1, d),
        "pool_w_out": pool_w_out[0].astype(BF16),
        "wr": jnp.pad(moe_w_router[0], ((0, 0), (0, LANES - ne))).astype(BF16),
        "br": jnp.pad(moe_b_router[0].astype(F32), (0, LANES - ne), constant_values=MASK_BIAS).reshape(1, LANES),
        "moe_wg": moe_w_gate[0].astype(BF16),
        "moe_wu": moe_w_up[0].astype(BF16),
        "moe_wd": moe_w_down[0].astype(BF16),
    }

    def group_mods(m, lo, n):
        return m[:, lo:lo + n, None, :]

    y_prompt = _trunk(x_prompt, group_mods(mods_mix, 0, bp), group_mods(mods_ffn, 0, bp), p)
    y_sample = _trunk(x_sample, group_mods(mods_mix, bp, bs), group_mods(mods_ffn, bp, bs), p)
    return (y_prompt, y_sample)
```

```python
import functools

import jax
import jax.numpy as jnp
from jax import lax
from jax.experimental import pallas as pl
from jax.experimental.pallas import tpu as pltpu

F32 = jnp.float32
BF16 = jnp.bfloat16

RMS_EPS = 1e-6
GRID_W = 64
NA_HEADS = 16
HEAD_DIM = 64
NA_KR = 8
NA_KW = 16
POOL_HALF = (1, 2, 4, 8)
N_EXPERTS = 8
TOP_K = 2
LANES = 128
SUBLANES = 8
BF16_ROWS = 16
MASK_BIAS = -1e30
ROW_PAIR = 2 * GRID_W
WIN_ROWS = NA_KR + 2
BIAS_TILES = 2 * NA_KR
VMEM_LIMIT = 48 * 1024 * 1024
TOKEN_TILE = 512
FF_TILE_DENSE = 896
FF_TILE_MOE = 1792


def _silu(a):
    return a * jax.nn.sigmoid(a)


def _mod_rmsnorm(x, g, shift, scale):
    var = jnp.mean(x * x, axis=-1, keepdims=True)
    y = (x * lax.rsqrt(var + RMS_EPS)) * g
    return y * (1.0 + scale) + shift


def _params(*sem, vmem=VMEM_LIMIT):
    return pltpu.CompilerParams(dimension_semantics=sem, vmem_limit_bytes=vmem)


def _largest_tile(total, target):
    best = LANES
    for cand in range(LANES, min(total, target) + 1, LANES):
        if total % cand == 0:
            best = cand
    assert total % best == 0
    return best


def _adaln_kernel(c_ref, w_ref, b_ref, o_ref):
    s = _silu(c_ref[...]).astype(BF16)
    o_ref[0] = jnp.dot(s, w_ref[0].astype(BF16), preferred_element_type=F32) + b_ref[0]


def _adaln(c_all, w, b):
    depth, d, d3 = w.shape
    r = c_all.shape[0]
    tn = _largest_tile(d3, 1024)
    return pl.pallas_call(
        _adaln_kernel,
        grid=(depth, d3 // tn),
        in_specs=[
            pl.BlockSpec((r, d), lambda i, j: (0, 0)),
            pl.BlockSpec((1, d, tn), lambda i, j: (i, 0, j)),
            pl.BlockSpec((1, 1, tn), lambda i, j: (i, 0, j)),
        ],
        out_specs=pl.BlockSpec((1, r, tn), lambda i, j: (i, 0, j)),
        out_shape=jax.ShapeDtypeStruct((depth, r, d3), F32),
        compiler_params=_params("parallel", "parallel"),
        name="adaln",
    )(c_all, w, b.reshape(depth, 1, d3))


def _qkv_kernel(x_ref, mod_ref, g_ref, wq_ref, wk_ref, wvt_ref, q_ref, k_ref, vt_ref):
    d = x_ref.shape[-1]
    h = _mod_rmsnorm(x_ref[0], g_ref[...], mod_ref[0, :, 0:d], mod_ref[0, :, d:2 * d]).astype(BF16)
    q = jnp.dot(h, wq_ref[...], preferred_element_type=F32) * (HEAD_DIM ** -0.5)
    q_ref[0] = q.astype(BF16)
    k_ref[0] = jnp.dot(h, wk_ref[...], preferred_element_type=F32).astype(BF16)
    vt = lax.dot_general(wvt_ref[...], h, (((1,), (1,)), ((), ())), preferred_element_type=F32)
    for j in range(vt_ref.shape[1]):
        vt_ref[0, j] = vt[:, j * LANES:(j + 1) * LANES].astype(BF16)


def _qkv(x, mod, g, wq, wk, wvt, tm):
    b, t, d = x.shape
    const = lambda i, j: (0, 0)
    return pl.pallas_call(
        _qkv_kernel,
        grid=(b, t // tm),
        in_specs=[
            pl.BlockSpec((1, tm, d), lambda i, j: (i, j, 0)),
            pl.BlockSpec((1, 1, 3 * d), lambda i, j: (i, 0, 0)),
            pl.BlockSpec((1, d), const),
            pl.BlockSpec((d, d), const),
            pl.BlockSpec((d, d), const),
            pl.BlockSpec((d, d), const),
        ],
        out_specs=[
            pl.BlockSpec((1, tm, d), lambda i, j: (i, j, 0)),
            pl.BlockSpec((1, tm, d), lambda i, j: (i, j, 0)),
            pl.BlockSpec((1, tm // LANES, d, LANES), lambda i, j: (i, j, 0, 0)),
        ],
        out_shape=[
            jax.ShapeDtypeStruct((b, t, d), BF16),
            jax.ShapeDtypeStruct((b, t, d), BF16),
            jax.ShapeDtypeStruct((b, t // LANES, d, LANES), BF16),
        ],
        compiler_params=_params("parallel", "parallel"),
        name="qkv",
    )(x, mod, g, wq, wk, wvt)


def _na_bias_table(rpb):
    h = rpb.shape[0]
    c = jnp.arange(GRID_W, dtype=jnp.int32)
    cs = jnp.clip(c - NA_KW // 2, 0, GRID_W - NA_KW)
    kc = c[:, None]
    valid = (kc >= cs[None, :]) & (kc < cs[None, :] + NA_KW)
    dc = jnp.clip(kc - c[None, :] + (NA_KW - 1), 0, 2 * NA_KW - 2)
    bt = jnp.where(valid, rpb.astype(F32)[:, :, dc], MASK_BIAS)
    bt = jnp.concatenate([bt, jnp.full((h, 1, GRID_W, GRID_W), MASK_BIAS, F32)], axis=1)
    bt = bt.reshape(h // 2, 2, BIAS_TILES, GRID_W, GRID_W).transpose(0, 2, 3, 1, 4)
    return bt.reshape(h // 2, BIAS_TILES, GRID_W, 2 * GRID_W)


def _na_kernel(q_ref, k_ref, vt_ref, bias_ref, o_ref, *, rows, pairs_per_step, row_pairs_per_step):
    step = pl.program_id(2)
    lane = lax.broadcasted_iota(jnp.int32, (ROW_PAIR, LANES), 1)
    first_head = lane < HEAD_DIM
    first_head_row = lax.broadcasted_iota(jnp.int32, (GRID_W, LANES), 1) < HEAD_DIM

    def row_pair(j, carry):
        rp = step * row_pairs_per_step + j
        r0 = 2 * rp
        wblk = jnp.clip(rp - NA_KR // 4, 0, (rows - WIN_ROWS) // 2)
        ws = 2 * wblk
        tok = pl.multiple_of(j * ROW_PAIR, ROW_PAIR)
        ktok = pl.multiple_of(ws * GRID_W, ROW_PAIR)
        for p in range(pairs_per_step):
            cols = slice(p * LANES, (p + 1) * LANES)
            q2 = q_ref[0, pl.ds(tok, ROW_PAIR), cols]
            qa = jnp.where(first_head, q2, jnp.zeros_like(q2))
            qb = jnp.where(first_head, jnp.zeros_like(q2), q2)
            qbd = jnp.concatenate([qa[:GRID_W], qb[:GRID_W], qa[GRID_W:], qb[GRID_W:]], axis=0)
            kwin = k_ref[0, pl.ds(ktok, WIN_ROWS * GRID_W), cols]
            st = lax.dot_general(kwin, qbd, (((1,), (1,)), ((), ())), preferred_element_type=F32)
            blocks = []
            for i in range(WIN_ROWS):
                kr = ws + i
                tiles = []
                for s in range(2):
                    r = r0 + s
                    rs = jnp.clip(r - NA_KR // 2, 0, rows - NA_KR)
                    inside = jnp.logical_and(kr >= rs, kr < rs + NA_KR)
                    idx = jnp.where(inside, kr - r + (NA_KR - 1), BIAS_TILES - 1)
                    tiles.append(bias_ref[p, idx])
                blocks.append(st[i * GRID_W:(i + 1) * GRID_W] + jnp.concatenate(tiles, axis=1))
            sc = jnp.concatenate(blocks, axis=0)
            m = jnp.max(sc, axis=0, keepdims=True)
            e = jnp.exp(sc - m)
            l = jnp.sum(e, axis=0, keepdims=True)
            pt = e.astype(BF16)
            ot = jnp.zeros((LANES, 2 * LANES), F32)
            for c in range(WIN_ROWS * GRID_W // LANES):
                vt = vt_ref[0, wblk + c, cols, :]
                ot = ot + jnp.dot(vt, pt[c * LANES:(c + 1) * LANES], preferred_element_type=F32)
            ot = ot / l
            outs = []
            for s in range(2):
                tt = ot[:, s * LANES:(s + 1) * LANES].T
                outs.append(jnp.where(first_head_row, tt[:GRID_W], tt[GRID_W:]))
            o_ref[0, pl.ds(tok, ROW_PAIR), cols] = jnp.concatenate(outs, axis=0).astype(BF16)
        return carry

    lax.fori_loop(0, row_pairs_per_step, row_pair, 0)


def _na(q, k, vt, bias, *, head_groups, row_pairs_per_step):
    b, t, d = q.shape
    rows = t // GRID_W
    assert rows % 2 == 0 and rows >= WIN_ROWS and (rows // 2) % row_pairs_per_step == 0
    gc = d // head_groups
    pairs = gc // LANES
    tq = row_pairs_per_step * ROW_PAIR
    kern = functools.partial(_na_kernel, rows=rows, pairs_per_step=pairs, row_pairs_per_step=row_pairs_per_step)
    return pl.pallas_call(
        kern,
        grid=(b, head_groups, t // tq),
        in_specs=[
            pl.BlockSpec((1, tq, gc), lambda i, g, j: (i, j, g)),
            pl.BlockSpec((1, t, gc), lambda i, g, j: (i, 0, g)),
            pl.BlockSpec((1, t // LANES, gc, LANES), lambda i, g, j: (i, 0, g, 0)),
            pl.BlockSpec((pairs, BIAS_TILES, GRID_W, LANES), lambda i, g, j: (g, 0, 0, 0)),
        ],
        out_specs=pl.BlockSpec((1, tq, gc), lambda i, g, j: (i, j, g)),
        out_shape=jax.ShapeDtypeStruct((b, t, d), BF16),
        compiler_params=_params("parallel", "parallel", "parallel"),
        name="na_attn",
    )(q, k, vt, bias)


def _proj_ffn_kernel(x_ref, o_ref, mm_ref, mf_ref, g_ref, wo_ref, wg_ref, wu_ref, wd_ref, out_ref, x1_s, h_s, acc_s):
    d = x_ref.shape[-1]
    f = pl.program_id(2)

    @pl.when(f == 0)
    def _():
        m = jnp.dot(o_ref[0], wo_ref[...], preferred_element_type=F32)
        x1 = x_ref[0] + mm_ref[0, :, 2 * d:3 * d] * m
        x1_s[...] = x1
        h_s[...] = _mod_rmsnorm(x1, g_ref[...], mf_ref[0, :, 0:d], mf_ref[0, :, d:2 * d]).astype(BF16)
        acc_s[...] = jnp.zeros_like(acc_s)

    h = h_s[...]
    a = jnp.dot(h, wg_ref[...], preferred_element_type=F32)
    u = jnp.dot(h, wu_ref[...], preferred_element_type=F32)
    acc_s[...] += jnp.dot((_silu(a) * u).astype(BF16), wd_ref[...], preferred_element_type=F32)

    @pl.when(f == pl.num_programs(2) - 1)
    def _():
        out_ref[0] = x1_s[...] + mf_ref[0, :, 2 * d:3 * d] * acc_s[...]


def _proj_ffn(x, o, mod_mix, mod_ffn, g, wo, wg, wu, wd, tm, tf):
    b, t, d = x.shape
    ff = wg.shape[1]
    tok = lambda i, j, f: (i, j, 0)
    seq = lambda i, j, f: (i, 0, 0)
    return pl.pallas_call(
        _proj_ffn_kernel,
        grid=(b, t // tm, ff // tf),
        in_specs=[
            pl.BlockSpec((1, tm, d), tok),
            pl.BlockSpec((1, tm, d), tok),
            pl.BlockSpec((1, 1, 3 * d), seq),
            pl.BlockSpec((1, 1, 3 * d), seq),
            pl.BlockSpec((1, d), lambda i, j, f: (0, 0)),
            pl.BlockSpec((d, d), lambda i, j, f: (0, 0)),
            pl.BlockSpec((d, tf), lambda i, j, f: (0, f)),
            pl.BlockSpec((d, tf), lambda i, j, f: (0, f)),
            pl.BlockSpec((tf, d), lambda i, j, f: (f, 0)),
        ],
        out_specs=pl.BlockSpec((1, tm, d), tok),
        out_shape=jax.ShapeDtypeStruct((b, t, d), F32),
        scratch_shapes=[pltpu.VMEM((tm, d), F32), pltpu.VMEM((tm, d), BF16), pltpu.VMEM((tm, d), F32)],
        compiler_params=_params("parallel", "parallel", "arbitrary"),
        name="proj_ffn",
    )(x, o, mod_mix, mod_ffn, g, wo, wg, wu, wd)


def _pool_kernel(x_ref, xp_ref, xn_ref, mod_ref, g_ref, win_ref, wgrp_ref, ps_ref, wout_ref, out_ref, *, seq_len):
    d = x_ref.shape[-1]
    tm = x_ref.shape[1]
    halo = SUBLANES
    n = tm + 2 * halo
    j = pl.program_id(1)
    x = x_ref[0]
    xe = jnp.concatenate([xp_ref[0], x, xn_ref[0]], axis=0)
    h = _mod_rmsnorm(xe, g_ref[...], mod_ref[0, :, 0:d], mod_ref[0, :, d:2 * d]).astype(BF16)
    u = jnp.dot(h, win_ref[...], preferred_element_type=F32)
    pos = j * tm - halo + lax.broadcasted_iota(jnp.int32, (n, 1), 0)
    u = jnp.where(jnp.logical_and(pos >= 0, pos < seq_len), u, 0.0)
    gc = d // len(POOL_HALF)
    ys = []
    for gi, half in enumerate(POOL_HALF):
        ug = u[:, gi * gc:(gi + 1) * gc]
        win = ug + pltpu.roll(ug, 1, axis=0)
        span = 1
        while span < half:
            win = pltpu.roll(win, span, axis=0) + pltpu.roll(win, n - span, axis=0)
            span *= 2
        cnt = jnp.clip(pos + half, 0, seq_len) - jnp.clip(pos - half, 0, seq_len)
        cnt = jnp.maximum(cnt, 1).astype(F32)
        pooled = (win / cnt - ug)[halo:halo + tm].astype(BF16)
        y = jnp.dot(pooled, wgrp_ref[gi], preferred_element_type=F32) * ps_ref[:, gi * gc:(gi + 1) * gc]
        ys.append(y.astype(BF16))
    m = jnp.dot(jnp.concatenate(ys, axis=1), wout_ref[...], preferred_element_type=F32)
    out_ref[0] = x + mod_ref[0, :, 2 * d:3 * d] * m


def _pool(x, mod, g, w_in, w_grp, pscale, w_out, tm):
    b, t, d = x.shape
    hb = tm // SUBLANES
    last = t // SUBLANES - 1
    const2 = lambda i, j: (0, 0)
    return pl.pallas_call(
        functools.partial(_pool_kernel, seq_len=t),
        grid=(b, t // tm),
        in_specs=[
            pl.BlockSpec((1, tm, d), lambda i, j: (i, j, 0)),
            pl.BlockSpec((1, SUBLANES, d), lambda i, j: (i, jnp.maximum(j * hb - 1, 0), 0)),
            pl.BlockSpec((1, SUBLANES, d), lambda i, j: (i, jnp.minimum((j + 1) * hb, last), 0)),
            pl.BlockSpec((1, 1, 3 * d), lambda i, j: (i, 0, 0)),
            pl.BlockSpec((1, d), const2),
            pl.BlockSpec((d, d), const2),
            pl.BlockSpec(w_grp.shape, lambda i, j: (0, 0, 0)),
            pl.BlockSpec((1, d), const2),
            pl.BlockSpec((d, d), const2),
        ],
        out_specs=pl.BlockSpec((1, tm, d), lambda i, j: (i, j, 0)),
        out_shape=jax.ShapeDtypeStruct((b, t, d), F32),
        compiler_params=_params("parallel", "parallel"),
        name="pool_mixer",
    )(x, x, x, mod, g, w_in, w_grp, pscale, w_out)


def _group_rows(tm):
    return TOP_K * tm + N_EXPERTS * BF16_ROWS


def _router_kernel(x_ref, mod_ref, g_ref, wr_ref, br_ref, h_ref, w_ref, lp_ref, seg_ref):
    d = x_ref.shape[-1]
    tm = x_ref.shape[1]
    h = _mod_rmsnorm(x_ref[0], g_ref[...], mod_ref[0, :, 0:d], mod_ref[0, :, d:2 * d]).astype(BF16)
    h_ref[0] = h
    logits = jnp.dot(h, wr_ref[...], preferred_element_type=F32) + br_ref[...]
    lane = lax.broadcasted_iota(jnp.int32, logits.shape, 1).astype(F32)
    v1 = jnp.max(logits, axis=-1, keepdims=True)
    i1 = jnp.min(jnp.where(logits == v1, lane, float(LANES)), axis=-1, keepdims=True)
    rest = jnp.where(lane == i1, -jnp.inf, logits)
    v2 = jnp.max(rest, axis=-1, keepdims=True)
    i2 = jnp.min(jnp.where(rest == v2, lane, float(LANES)), axis=-1, keepdims=True)
    e2 = jnp.exp(v2 - v1)
    w1 = 1.0 / (1.0 + e2)
    w2 = e2 / (1.0 + e2)
    sel = jnp.where(lane == i1, 1.0, jnp.where(lane == i2, 1.0, 0.0))
    row = lax.broadcasted_iota(jnp.int32, (tm, tm), 0)
    col = lax.broadcasted_iota(jnp.int32, (tm, tm), 1)
    before = jnp.where(col < row, 1.0, 0.0).astype(BF16)
    rank = jnp.dot(before, sel.astype(BF16), preferred_element_type=F32)
    count = jnp.sum(sel, axis=0, keepdims=True)
    tiles = jnp.floor((count + (BF16_ROWS - 1)) * (1.0 / BF16_ROWS))
    er = lax.broadcasted_iota(jnp.int32, (LANES, LANES), 0)
    ec = lax.broadcasted_iota(jnp.int32, (LANES, LANES), 1)
    lower = jnp.where(er < ec, 1.0, 0.0).astype(BF16)
    tiles8 = jnp.broadcast_to(tiles, (SUBLANES, LANES)).astype(BF16)
    start = jnp.dot(tiles8, lower, preferred_element_type=F32)[0:1] * BF16_ROWS
    slot = start + rank
    lp1 = jnp.sum(jnp.where(lane == i1, slot, 0.0), axis=-1, keepdims=True)
    lp2 = jnp.sum(jnp.where(lane == i2, slot, 0.0), axis=-1, keepdims=True)
    w_ref[0] = jnp.where(lane == 0, w1, jnp.where(lane == 1, w2, 0.0))
    lp_ref[0] = jnp.where(lane == 0, lp1, jnp.where(lane == 1, lp2, 0.0))
    sub = lax.broadcasted_iota(jnp.int32, (SUBLANES, LANES), 0)
    seg = jnp.where(sub == 0, tiles * BF16_ROWS, jnp.where(sub == 1, start, 0.0))
    seg_ref[0] = seg.astype(jnp.int32)


def _router(x, mod, g, wr, br, tm):
    b, t, d = x.shape
    nt = t // tm
    const2 = lambda i, j: (0, 0)
    tok = lambda i, j: (i, j, 0)
    return pl.pallas_call(
        _router_kernel,
        grid=(b, nt),
        in_specs=[
            pl.BlockSpec((1, tm, d), tok),
            pl.BlockSpec((1, 1, 3 * d), lambda i, j: (i, 0, 0)),
            pl.BlockSpec((1, d), const2),
            pl.BlockSpec((d, LANES), const2),
            pl.BlockSpec((1, LANES), const2),
        ],
        out_specs=[
            pl.BlockSpec((1, tm, d), tok),
            pl.BlockSpec((1, tm, LANES), tok),
            pl.BlockSpec((1, tm, LANES), tok),
            pl.BlockSpec((1, SUBLANES, LANES), lambda i, j: (i * nt + j, 0, 0)),
        ],
        out_shape=[
            jax.ShapeDtypeStruct((b, t, d), BF16),
            jax.ShapeDtypeStruct((b, t, LANES), F32),
            jax.ShapeDtypeStruct((b, t, LANES), F32),
            jax.ShapeDtypeStruct((b * nt, SUBLANES, LANES), jnp.int32),
        ],
        compiler_params=_params("parallel", "parallel"),
        name="router",
    )(x, mod, g, wr, br)


def _segment_copies(tile, dst_ref, n_ref, ls_ref, make_copy, act):
    for e in range(N_EXPERTS):
        n = n_ref[tile * N_EXPERTS + e]
        ls = ls_ref[tile * N_EXPERTS + e]
        dst = dst_ref[tile * N_EXPERTS + e]
        size = BF16_ROWS
        while size <= TOKEN_TILE:
            done = n & (-2 * size)

            @pl.when((n & size) != 0)
            def _(size=size, done=done, ls=ls, dst=dst):
                act(make_copy(pl.multiple_of(ls + done, BF16_ROWS), pl.multiple_of(dst + done, BF16_ROWS), size))

            size *= 2


def _dispatch_kernel(dst_ref, n_ref, ls_ref, h_ref, lp_ref, xs_in_ref, xs_ref, grp_s, sem):
    del xs_in_ref
    tile = pl.program_id(0)
    tm = h_ref.shape[0]
    gb = grp_s.shape[0]
    lpt = lp_ref[...].T
    slot = lax.broadcasted_iota(jnp.int32, (gb, tm), 0).astype(F32)
    onehot = jnp.where(slot == lpt[0:1, :], 1.0, jnp.where(slot == lpt[1:2, :], 1.0, 0.0)).astype(BF16)
    grp_s[...] = jnp.dot(onehot, h_ref[...], preferred_element_type=F32).astype(BF16)

    def make_copy(local, glob, size):
        return pltpu.make_async_copy(grp_s.at[pl.ds(local, size), :], xs_ref.at[pl.ds(glob, size), :], sem)

    _segment_copies(tile, dst_ref, n_ref, ls_ref, make_copy, lambda c: c.start())
    _segment_copies(tile, dst_ref, n_ref, ls_ref, make_copy, lambda c: c.wait())


def _dispatch(h, lp, dst, seg_n, seg_ls, slots, tm):
    n, d = h.shape
    assert tm == TOKEN_TILE
    grid_spec = pltpu.PrefetchScalarGridSpec(
        num_scalar_prefetch=3,
        grid=(n // tm,),
        in_specs=[
            pl.BlockSpec((tm, d), lambda i, *_: (i, 0)),
            pl.BlockSpec((tm, LANES), lambda i, *_: (i, 0)),
            pl.BlockSpec(memory_space=pl.ANY),
        ],
        out_specs=pl.BlockSpec(memory_space=pl.ANY),
        scratch_shapes=[pltpu.VMEM((_group_rows(tm), d), BF16), pltpu.SemaphoreType.DMA(())],
    )
    return pl.pallas_call(
        _dispatch_kernel,
        grid_spec=grid_spec,
        out_shape=jax.ShapeDtypeStruct((slots, d), BF16),
        input_output_aliases={5: 0},
        compiler_params=_params("arbitrary"),
        name="moe_dispatch",
    )(dst, seg_n, seg_ls, h, lp, jnp.zeros((slots, d), BF16))


def _experts_kernel(te_ref, tv_ref, xs_ref, wg_ref, wu_ref, wd_ref, ys_ref, acc_s):
    i = pl.program_id(0)
    f = pl.program_id(1)
    last = pl.num_programs(1) - 1
    live = tv_ref[i] > 0

    @pl.when(jnp.logical_and(live, f == 0))
    def _():
        acc_s[...] = jnp.zeros_like(acc_s)

    @pl.when(live)
    def _():
        h = xs_ref[...]
        a = jnp.dot(h, wg_ref[0], preferred_element_type=F32)
        u = jnp.dot(h, wu_ref[0], preferred_element_type=F32)
        acc_s[...] += jnp.dot((_silu(a) * u).astype(BF16), wd_ref[0], preferred_element_type=F32)

    @pl.when(jnp.logical_and(live, f == last))
    def _():
        ys_ref[...] = acc_s[...].astype(BF16)

    @pl.when(jnp.logical_and(jnp.logical_not(live), f == last))
    def _():
        ys_ref[...] = jnp.zeros_like(ys_ref)


def _experts(xs, tile_expert, tile_live, wg, wu, wd, tm, tf):
    slots, d = xs.shape
    ff = wg.shape[2]
    nf = ff // tf
    fidx = lambda f, tv, i: f * tv[i] + (nf - 1) * (1 - tv[i])
    grid_spec = pltpu.PrefetchScalarGridSpec(
        num_scalar_prefetch=2,
        grid=(slots // tm, nf),
        in_specs=[
            pl.BlockSpec((tm, d), lambda i, f, te, tv: (i, 0)),
            pl.BlockSpec((1, d, tf), lambda i, f, te, tv: (te[i], 0, fidx(f, tv, i))),
            pl.BlockSpec((1, d, tf), lambda i, f, te, tv: (te[i], 0, fidx(f, tv, i))),
            pl.BlockSpec((1, tf, d), lambda i, f, te, tv: (te[i], fidx(f, tv, i), 0)),
        ],
        out_specs=pl.BlockSpec((tm, d), lambda i, f, te, tv: (i, 0)),
        scratch_shapes=[pltpu.VMEM((tm, d), F32)],
    )
    return pl.pallas_call(
        _experts_kernel,
        grid_spec=grid_spec,
        out_shape=jax.ShapeDtypeStruct((slots, d), BF16),
        compiler_params=_params("parallel", "arbitrary"),
        name="moe_experts",
    )(tile_expert, tile_live, xs, wg, wu, wd)


def _combine_kernel(dst_ref, n_ref, ls_ref, x_ref, w_ref, lp_ref, mod_ref, gf_ref, ys_ref, out_ref, y_s, sem):
    d = x_ref.shape[-1]
    tm = x_ref.shape[1]
    gb = y_s.shape[0]
    tile = pl.program_id(0) * pl.num_programs(1) + pl.program_id(1)
    y_s[TOP_K * tm:, :] = jnp.zeros((gb - TOP_K * tm, d), BF16)

    def make_copy(local, glob, size):
        return pltpu.make_async_copy(ys_ref.at[pl.ds(glob, size), :], y_s.at[pl.ds(local, size), :], sem)

    _segment_copies(tile, dst_ref, n_ref, ls_ref, make_copy, lambda c: c.start())
    _segment_copies(tile, dst_ref, n_ref, ls_ref, make_copy, lambda c: c.wait())

    y = y_s[...]
    lp = lp_ref[0]
    w = w_ref[0]
    slot = lax.broadcasted_iota(jnp.int32, (tm, gb), 1).astype(F32)
    y1 = jnp.dot(jnp.where(slot == lp[:, 0:1], 1.0, 0.0).astype(BF16), y, preferred_element_type=F32)
    y2 = jnp.dot(jnp.where(slot == lp[:, 1:2], 1.0, 0.0).astype(BF16), y, preferred_element_type=F32)
    f = w[:, 0:1] * y1 + w[:, 1:2] * y2
    x = x_ref[0] + mod_ref[0, :, 2 * d:3 * d] * f
    var = jnp.mean(x * x, axis=-1, keepdims=True)
    out_ref[0] = (x * lax.rsqrt(var + RMS_EPS)) * gf_ref[...]


def _combine(x, w, lp, mod, gf, ys, dst, seg_n, seg_ls, tm):
    b, t, d = x.shape
    assert tm == TOKEN_TILE
    tok = lambda i, j, *_: (i, j, 0)
    grid_spec = pltpu.PrefetchScalarGridSpec(
        num_scalar_prefetch=3,
        grid=(b, t // tm),
        in_specs=[
            pl.BlockSpec((1, tm, d), tok),
            pl.BlockSpec((1, tm, LANES), tok),
            pl.BlockSpec((1, tm, LANES), tok),
            pl.BlockSpec((1, 1, 3 * d), lambda i, j, *_: (i, 0, 0)),
            pl.BlockSpec((1, d), lambda i, j, *_: (0, 0)),
            pl.BlockSpec(memory_space=pl.ANY),
        ],
        out_specs=pl.BlockSpec((1, tm, d), tok),
        scratch_shapes=[pltpu.VMEM((_group_rows(tm), d), BF16), pltpu.SemaphoreType.DMA(())],
    )
    return pl.pallas_call(
        _combine_kernel,
        grid_spec=grid_spec,
        out_shape=jax.ShapeDtypeStruct((b, t, d), F32),
        compiler_params=_params("arbitrary", "arbitrary"),
        name="moe_combine",
    )(dst, seg_n, seg_ls, x, w, lp, mod, gf, ys)


def _trunk(x, mods_mix, mods_ffn, p):
    b, t, d = x.shape
    tm = TOKEN_TILE
    assert t % tm == 0
    ff = p["ffn_wg"].shape[1]

    q, k, vt = _qkv(x, mods_mix[0], p["ln_mix_g"][0], p["wq"], p["wk"], p["wvt"], tm)
    rows = t // GRID_W
    rpp = 4 if (rows // 2) % 4 == 0 else 1
    o = _na(q, k, vt, p["na_bias"], head_groups=2, row_pairs_per_step=rpp)
    x = _proj_ffn(x, o, mods_mix[0], mods_ffn[0], p["ln_ffn_g"][0], p["wo"], p["ffn_wg"], p["ffn_wu"], p["ffn_wd"],
                  tm, _largest_tile(ff, FF_TILE_DENSE))

    x = _pool(x, mods_mix[1], p["ln_mix_g"][1], p["pool_w_in"], p["pool_w_grp"], p["pool_scale"], p["pool_w_out"], tm)
    h, w, lp, seg = _router(x, mods_ffn[1], p["ln_ffn_g"][1], p["wr"], p["br"], tm)

    n = b * t
    nt = n // tm
    seg_n = seg[:, 0, :N_EXPERTS]
    seg_ls = seg[:, 1, :N_EXPERTS]
    before = jnp.cumsum(seg_n, axis=0) - seg_n
    total = jnp.sum(seg_n, axis=0)
    group = ((total + tm - 1) // tm) * tm
    ends = jnp.cumsum(group)
    dst = (ends - group)[None, :] + before
    n_tiles = -(-(TOP_K * n + nt * N_EXPERTS * (BF16_ROWS - 1)) // tm) + N_EXPERTS
    tile_start = jnp.arange(n_tiles, dtype=jnp.int32) * tm
    tile_live = (tile_start < ends[-1]).astype(jnp.int32)
    tile_expert = jnp.sum((ends[None, :] <= tile_start[:, None]).astype(jnp.int32), axis=1)
    last_expert = jnp.sum((ends <= ends[-1] - 1).astype(jnp.int32))
    tile_expert = jnp.minimum(jnp.where(tile_live > 0, tile_expert, last_expert), N_EXPERTS - 1)
    dst, seg_n, seg_ls = (a.reshape(-1).astype(jnp.int32) for a in (dst, seg_n, seg_ls))

    xs = _dispatch(h.reshape(n, d), lp.reshape(n, LANES), dst, seg_n, seg_ls, n_tiles * tm, tm)
    ys = _experts(xs, tile_expert, tile_live, p["moe_wg"], p["moe_wu"], p["moe_wd"], tm, _largest_tile(ff, FF_TILE_MOE))
    return _combine(x, w, lp, mods_ffn[1], p["ln_f_g"], ys, dst, seg_n, seg_ls, tm)


def kernel(x_prompt, x_sample, c_prompt, c_sample, ln_mix_g, ada_mix_w, ada_mix_b, ln_ffn_g, ada_ffn_w, ada_ffn_b, na_w_qkv, na_rpb, na_w_o, pool_w_in, pool_w_grp, pool_scale, pool_w_out, ffn_w_gate, ffn_w_up, ffn_w_down, moe_w_router, moe_b_router, moe_w_gate, moe_w_up, moe_w_down, ln_f_g):
    d = x_prompt.shape[-1]
    depth = ln_mix_g.shape[0]
    assert depth == 2 and d == NA_HEADS * HEAD_DIM
    bp, bs = c_prompt.shape[0], c_sample.shape[0]

    pad = (-(bp + bs)) % SUBLANES
    c_all = jnp.concatenate([c_prompt, c_sample, jnp.zeros((pad, d), F32)], axis=0)
    mods_mix = _adaln(c_all, ada_mix_w, ada_mix_b)
    mods_ffn = _adaln(c_all, ada_ffn_w, ada_ffn_b)

    wqkv = na_w_qkv[0]
    ne = moe_w_router.shape[-1]
    assert ne == N_EXPERTS
    p = {
        "ln_mix_g": ln_mix_g.reshape(depth, 1, d),
        "ln_ffn_g": ln_ffn_g.reshape(depth, 1, d),
        "ln_f_g": ln_f_g.reshape(1, d),
        "wq": wqkv[:, 0:d].astype(BF16),
        "wk": wqkv[:, d:2 * d].astype(BF16),
        "wvt": wqkv[:, 2 * d:3 * d].T.astype(BF16),
        "na_bias": _na_bias_table(na_rpb[0]),
        "wo": na_w_o[0].astype(BF16),
        "ffn_wg": ffn_w_gate[0].astype(BF16),
        "ffn_wu": ffn_w_up[0].astype(BF16),
        "ffn_wd": ffn_w_down[0].astype(BF16),
        "pool_w_in": pool_w_in[0].astype(BF16),
        "pool_w_grp": pool_w_grp[0].astype(BF16),
        "pool_scale": pool_scale[0].reshape(1, d),
        "pool_w_out": pool_w_out[0].astype(BF16),
        "wr": jnp.pad(moe_w_router[0], ((0, 0), (0, LANES - ne))).astype(BF16),
        "br": jnp.pad(moe_b_router[0].astype(F32), (0, LANES - ne), constant_values=MASK_BIAS).reshape(1, LANES),
        "moe_wg": moe_w_gate[0].astype(BF16),
        "moe_wu": moe_w_up[0].astype(BF16),
        "moe_wd": moe_w_down[0].astype(BF16),
    }

    def group_mods(m, lo, n):
        return m[:, lo:lo + n, None, :]

    y_prompt = _trunk(x_prompt, group_mods(mods_mix, 0, bp), group_mods(mods_ffn, 0, bp), p)
    y_sample = _trunk(x_sample, group_mods(mods_mix, bp, bs), group_mods(mods_ffn, bp, bs), p)
    return (y_prompt, y_sample)
```

```python
import functools

import jax
import jax.numpy as jnp
from jax import lax
from jax.experimental import pallas as pl
from jax.experimental.pallas import tpu as pltpu

F32 = jnp.float32
BF16 = jnp.bfloat16

RMS_EPS = 1e-6
GRID_W = 64
NA_HEADS = 16
HEAD_DIM = 64
NA_KR = 8
NA_KW = 16
POOL_HALF = (1, 2, 4, 8)
N_EXPERTS = 8
TOP_K = 2
LANES = 128
SUBLANES = 8
BF16_ROWS = 16
MASK_BIAS = -1e30
LOG2_E = 1.4426950408889634
ROW_PAIR = 2 * GRID_W
WIN_ROWS = NA_KR + 2
BIAS_TILES = 2 * NA_KR
VMEM_LIMIT = 56 * 1024 * 1024
TOKEN_TILE = 512
FF_TILE_DENSE = 1792
FF_TILE_MOE = 1792


def _silu(a):
    return a * jax.nn.sigmoid(a)


def _mod_rmsnorm(x, g, shift, scale):
    var = jnp.mean(x * x, axis=-1, keepdims=True)
    y = (x * lax.rsqrt(var + RMS_EPS)) * g
    return y * (1.0 + scale) + shift


def _params(*sem, vmem=VMEM_LIMIT):
    return pltpu.CompilerParams(dimension_semantics=sem, vmem_limit_bytes=vmem)


def _largest_tile(total, target):
    best = LANES
    for cand in range(LANES, min(total, target) + 1, LANES):
        if total % cand == 0:
            best = cand
    assert total % best == 0
    return best


def _adaln_kernel(c_ref, w_ref, b_ref, o_ref):
    s = _silu(c_ref[...]).astype(BF16)
    o_ref[0] = jnp.dot(s, w_ref[0].astype(BF16), preferred_element_type=F32) + b_ref[0]


def _adaln(c_all, w, b):
    depth, d, d3 = w.shape
    r = c_all.shape[0]
    tn = _largest_tile(d3, 1024)
    return pl.pallas_call(
        _adaln_kernel,
        grid=(depth, d3 // tn),
        in_specs=[
            pl.BlockSpec((r, d), lambda i, j: (0, 0)),
            pl.BlockSpec((1, d, tn), lambda i, j: (i, 0, j)),
            pl.BlockSpec((1, 1, tn), lambda i, j: (i, 0, j)),
        ],
        out_specs=pl.BlockSpec((1, r, tn), lambda i, j: (i, 0, j)),
        out_shape=jax.ShapeDtypeStruct((depth, r, d3), F32),
        compiler_params=_params("parallel", "parallel"),
        name="adaln",
    )(c_all, w, b.reshape(depth, 1, d3))


def _qkv_kernel(x_ref, mod_ref, g_ref, wq_ref, wk_ref, wvt_ref, q_ref, k_ref, vt_ref):
    d = x_ref.shape[-1]
    h = _mod_rmsnorm(x_ref[0], g_ref[...], mod_ref[0, :, 0:d], mod_ref[0, :, d:2 * d]).astype(BF16)
    q = jnp.dot(h, wq_ref[...], preferred_element_type=F32) * (HEAD_DIM ** -0.5 * LOG2_E)
    q_ref[0] = q.astype(BF16)
    k_ref[0] = jnp.dot(h, wk_ref[...], preferred_element_type=F32).astype(BF16)
    vt = lax.dot_general(wvt_ref[...], h, (((1,), (1,)), ((), ())), preferred_element_type=F32)
    for j in range(vt_ref.shape[1]):
        vt_ref[0, j] = vt[:, j * LANES:(j + 1) * LANES].astype(BF16)


def _qkv(x, mod, g, wq, wk, wvt, tm):
    b, t, d = x.shape
    const = lambda i, j: (0, 0)
    return pl.pallas_call(
        _qkv_kernel,
        grid=(b, t // tm),
        in_specs=[
            pl.BlockSpec((1, tm, d), lambda i, j: (i, j, 0)),
            pl.BlockSpec((1, 1, 3 * d), lambda i, j: (i, 0, 0)),
            pl.BlockSpec((1, d), const),
            pl.BlockSpec((d, d), const),
            pl.BlockSpec((d, d), const),
            pl.BlockSpec((d, d), const),
        ],
        out_specs=[
            pl.BlockSpec((1, tm, d), lambda i, j: (i, j, 0)),
            pl.BlockSpec((1, tm, d), lambda i, j: (i, j, 0)),
            pl.BlockSpec((1, tm // LANES, d, LANES), lambda i, j: (i, j, 0, 0)),
        ],
        out_shape=[
            jax.ShapeDtypeStruct((b, t, d), BF16),
            jax.ShapeDtypeStruct((b, t, d), BF16),
            jax.ShapeDtypeStruct((b, t // LANES, d, LANES), BF16),
        ],
        compiler_params=_params("parallel", "parallel"),
        name="qkv",
    )(x, mod, g, wq, wk, wvt)


def _na_bias_table(rpb):
    h = rpb.shape[0]
    c = jnp.arange(GRID_W, dtype=jnp.int32)
    cs = jnp.clip(c - NA_KW // 2, 0, GRID_W - NA_KW)
    kc = c[:, None]
    valid = (kc >= cs[None, :]) & (kc < cs[None, :] + NA_KW)
    dc = jnp.clip(kc - c[None, :] + (NA_KW - 1), 0, 2 * NA_KW - 2)
    bt = jnp.where(valid, rpb.astype(F32)[:, :, dc], MASK_BIAS)
    bt = jnp.concatenate([bt, jnp.full((h, 1, GRID_W, GRID_W), MASK_BIAS, F32)], axis=1)
    bt = bt.reshape(h // 2, 2, BIAS_TILES, GRID_W, GRID_W).transpose(0, 2, 3, 1, 4)
    return bt.reshape(h // 2, BIAS_TILES, GRID_W, 2 * GRID_W) * LOG2_E


def _na_kernel(q_ref, k_ref, vt_ref, bias_ref, o_ref, *, rows, pairs_per_step, row_pairs_per_step):
    step = pl.program_id(2)
    lane = lax.broadcasted_iota(jnp.int32, (ROW_PAIR, LANES), 1)
    first_head = lane < HEAD_DIM
    first_head_row = lax.broadcasted_iota(jnp.int32, (GRID_W, LANES), 1) < HEAD_DIM

    def row_pair(j, carry):
        rp = step * row_pairs_per_step + j
        r0 = 2 * rp
        wblk = jnp.clip(rp - NA_KR // 4, 0, (rows - WIN_ROWS) // 2)
        ws = 2 * wblk
        tok = j * ROW_PAIR
        ktok = pl.multiple_of(ws * GRID_W, ROW_PAIR)
        for p in range(pairs_per_step):
            cols = slice(p * LANES, (p + 1) * LANES)
            q2 = q_ref[0, pl.ds(tok, ROW_PAIR), cols]
            qa = jnp.where(first_head, q2, jnp.zeros_like(q2))
            qb = jnp.where(first_head, jnp.zeros_like(q2), q2)
            qbd = jnp.concatenate([qa[:GRID_W], qb[:GRID_W], qa[GRID_W:], qb[GRID_W:]], axis=0)
            kwin = k_ref[0, pl.ds(ktok, WIN_ROWS * GRID_W), cols]
            st = lax.dot_general(kwin, qbd, (((1,), (1,)), ((), ())), preferred_element_type=F32)
            blocks = []
            for i in range(WIN_ROWS):
                kr = ws + i
                tiles = []
                for s in range(2):
                    r = r0 + s
                    rs = jnp.clip(r - NA_KR // 2, 0, rows - NA_KR)
                    inside = jnp.logical_and(kr >= rs, kr < rs + NA_KR)
                    idx = jnp.where(inside, kr - r + (NA_KR - 1), BIAS_TILES - 1)
                    tiles.append(bias_ref[p, idx])
                blocks.append(st[i * GRID_W:(i + 1) * GRID_W] + jnp.concatenate(tiles, axis=1))
            sc = jnp.concatenate(blocks, axis=0)
            m = jnp.max(sc, axis=0, keepdims=True)
            e = jnp.exp2(sc - m)
            l = jnp.sum(e, axis=0, keepdims=True)
            pt = e.astype(BF16)
            ot = jnp.zeros((LANES, 2 * LANES), F32)
            for c in range(WIN_ROWS * GRID_W // LANES):
                vt = vt_ref[0, wblk + c, cols, :]
                ot = ot + jnp.dot(vt, pt[c * LANES:(c + 1) * LANES], preferred_element_type=F32)
            ot = ot * (1.0 / l)
            outs = []
            for s in range(2):
                tt = ot[:, s * LANES:(s + 1) * LANES].T
                outs.append(jnp.where(first_head_row, tt[:GRID_W], tt[GRID_W:]))
            o_ref[0, pl.ds(tok, ROW_PAIR), cols] = jnp.concatenate(outs, axis=0).astype(BF16)
        return carry

    for j in range(row_pairs_per_step):
        row_pair(j, 0)


def _na(q, k, vt, bias, *, head_groups, row_pairs_per_step):
    b, t, d = q.shape
    rows = t // GRID_W
    assert rows % 2 == 0 and rows >= WIN_ROWS and (rows // 2) % row_pairs_per_step == 0
    gc = d // head_groups
    pairs = gc // LANES
    tq = row_pairs_per_step * ROW_PAIR
    kern = functools.partial(_na_kernel, rows=rows, pairs_per_step=pairs, row_pairs_per_step=row_pairs_per_step)
    return pl.pallas_call(
        kern,
        grid=(b, head_groups, t // tq),
        in_specs=[
            pl.BlockSpec((1, tq, gc), lambda i, g, j: (i, j, g)),
            pl.BlockSpec((1, t, gc), lambda i, g, j: (i, 0, g)),
            pl.BlockSpec((1, t // LANES, gc, LANES), lambda i, g, j: (i, 0, g, 0)),
            pl.BlockSpec((pairs, BIAS_TILES, GRID_W, LANES), lambda i, g, j: (g, 0, 0, 0)),
        ],
        out_specs=pl.BlockSpec((1, tq, gc), lambda i, g, j: (i, j, g)),
        out_shape=jax.ShapeDtypeStruct((b, t, d), BF16),
        compiler_params=_params("parallel", "parallel", "parallel"),
        name="na_attn",
    )(q, k, vt, bias)


def _proj_ffn_kernel(x_ref, o_ref, mm_ref, mf_ref, g_ref, wo_ref, wg_ref, wu_ref, wd_ref, out_ref, x1_s, h_s, acc_s):
    d = x_ref.shape[-1]
    f = pl.program_id(2)

    @pl.when(f == 0)
    def _():
        m = jnp.dot(o_ref[0], wo_ref[...], preferred_element_type=F32)
        x1 = x_ref[0] + mm_ref[0, :, 2 * d:3 * d] * m
        x1_s[...] = x1
        h_s[...] = _mod_rmsnorm(x1, g_ref[...], mf_ref[0, :, 0:d], mf_ref[0, :, d:2 * d]).astype(BF16)
        acc_s[...] = jnp.zeros_like(acc_s)

    h = h_s[...]
    a = jnp.dot(h, wg_ref[...], preferred_element_type=F32)
    u = jnp.dot(h, wu_ref[...], preferred_element_type=F32)
    acc_s[...] += jnp.dot((_silu(a) * u).astype(BF16), wd_ref[...], preferred_element_type=F32)

    @pl.when(f == pl.num_programs(2) - 1)
    def _():
        out_ref[0] = x1_s[...] + mf_ref[0, :, 2 * d:3 * d] * acc_s[...]


def _proj_ffn(x, o, mod_mix, mod_ffn, g, wo, wg, wu, wd, tm, tf):
    b, t, d = x.shape
    ff = wg.shape[1]
    tok = lambda i, j, f: (i, j, 0)
    seq = lambda i, j, f: (i, 0, 0)
    return pl.pallas_call(
        _proj_ffn_kernel,
        grid=(b, t // tm, ff // tf),
        in_specs=[
            pl.BlockSpec((1, tm, d), tok),
            pl.BlockSpec((1, tm, d), tok),
            pl.BlockSpec((1, 1, 3 * d), seq),
            pl.BlockSpec((1, 1, 3 * d), seq),
            pl.BlockSpec((1, d), lambda i, j, f: (0, 0)),
            pl.BlockSpec((d, d), lambda i, j, f: (0, 0)),
            pl.BlockSpec((d, tf), lambda i, j, f: (0, f)),
            pl.BlockSpec((d, tf), lambda i, j, f: (0, f)),
            pl.BlockSpec((tf, d), lambda i, j, f: (f, 0)),
        ],
        out_specs=pl.BlockSpec((1, tm, d), tok),
        out_shape=jax.ShapeDtypeStruct((b, t, d), F32),
        scratch_shapes=[pltpu.VMEM((tm, d), F32), pltpu.VMEM((tm, d), BF16), pltpu.VMEM((tm, d), F32)],
        compiler_params=_params("parallel", "parallel", "arbitrary"),
        name="proj_ffn",
    )(x, o, mod_mix, mod_ffn, g, wo, wg, wu, wd)


def _pool_kernel(x_ref, xp_ref, xn_ref, mod_ref, g_ref, win_ref, wgrp_ref, ps_ref, wout_ref, out_ref, *, seq_len):
    d = x_ref.shape[-1]
    tm = x_ref.shape[1]
    halo = SUBLANES
    n = tm + 2 * halo
    j = pl.program_id(1)
    x = x_ref[0]
    xe = jnp.concatenate([xp_ref[0], x, xn_ref[0]], axis=0)
    h = _mod_rmsnorm(xe, g_ref[...], mod_ref[0, :, 0:d], mod_ref[0, :, d:2 * d]).astype(BF16)
    u = jnp.dot(h, win_ref[...], preferred_element_type=F32)
    pos = j * tm - halo + lax.broadcasted_iota(jnp.int32, (n, 1), 0)
    u = jnp.where(jnp.logical_and(pos >= 0, pos < seq_len), u, 0.0)
    gc = d // len(POOL_HALF)
    ys = []
    for gi, half in enumerate(POOL_HALF):
        ug = u[:, gi * gc:(gi + 1) * gc]
        win = ug + pltpu.roll(ug, 1, axis=0)
        span = 1
        while span < half:
            win = pltpu.roll(win, span, axis=0) + pltpu.roll(win, n - span, axis=0)
            span *= 2
        cnt = jnp.clip(pos + half, 0, seq_len) - jnp.clip(pos - half, 0, seq_len)
        cnt = jnp.maximum(cnt, 1).astype(F32)
        pooled = (win / cnt - ug)[halo:halo + tm].astype(BF16)
        y = jnp.dot(pooled, wgrp_ref[gi], preferred_element_type=F32) * ps_ref[:, gi * gc:(gi + 1) * gc]
        ys.append(y.astype(BF16))
    m = jnp.dot(jnp.concatenate(ys, axis=1), wout_ref[...], preferred_element_type=F32)
    out_ref[0] = x + mod_ref[0, :, 2 * d:3 * d] * m


def _pool(x, mod, g, w_in, w_grp, pscale, w_out, tm):
    b, t, d = x.shape
    hb = tm // SUBLANES
    last = t // SUBLANES - 1
    const2 = lambda i, j: (0, 0)
    return pl.pallas_call(
        functools.partial(_pool_kernel, seq_len=t),
        grid=(b, t // tm),
        in_specs=[
            pl.BlockSpec((1, tm, d), lambda i, j: (i, j, 0)),
            pl.BlockSpec((1, SUBLANES, d), lambda i, j: (i, jnp.maximum(j * hb - 1, 0), 0)),
            pl.BlockSpec((1, SUBLANES, d), lambda i, j: (i, jnp.minimum((j + 1) * hb, last), 0)),
            pl.BlockSpec((1, 1, 3 * d), lambda i, j: (i, 0, 0)),
            pl.BlockSpec((1, d), const2),
            pl.BlockSpec((d, d), const2),
            pl.BlockSpec(w_grp.shape, lambda i, j: (0, 0, 0)),
            pl.BlockSpec((1, d), const2),
            pl.BlockSpec((d, d), const2),
        ],
        out_specs=pl.BlockSpec((1, tm, d), lambda i, j: (i, j, 0)),
        out_shape=jax.ShapeDtypeStruct((b, t, d), F32),
        compiler_params=_params("parallel", "parallel"),
        name="pool_mixer",
    )(x, x, x, mod, g, w_in, w_grp, pscale, w_out)


def _group_rows(tm):
    return TOP_K * tm + N_EXPERTS * BF16_ROWS


def _router_kernel(x_ref, mod_ref, g_ref, wr_ref, br_ref, h_ref, w_ref, lp_ref, seg_ref):
    d = x_ref.shape[-1]
    tm = x_ref.shape[1]
    h = _mod_rmsnorm(x_ref[0], g_ref[...], mod_ref[0, :, 0:d], mod_ref[0, :, d:2 * d]).astype(BF16)
    h_ref[0] = h
    logits = jnp.dot(h, wr_ref[...], preferred_element_type=F32) + br_ref[...]
    lane = lax.broadcasted_iota(jnp.int32, logits.shape, 1).astype(F32)
    v1 = jnp.max(logits, axis=-1, keepdims=True)
    i1 = jnp.min(jnp.where(logits == v1, lane, float(LANES)), axis=-1, keepdims=True)
    rest = jnp.where(lane == i1, -jnp.inf, logits)
    v2 = jnp.max(rest, axis=-1, keepdims=True)
    i2 = jnp.min(jnp.where(rest == v2, lane, float(LANES)), axis=-1, keepdims=True)
    e2 = jnp.exp(v2 - v1)
    w1 = 1.0 / (1.0 + e2)
    w2 = e2 / (1.0 + e2)
    sel = jnp.where(lane == i1, 1.0, jnp.where(lane == i2, 1.0, 0.0))
    row = lax.broadcasted_iota(jnp.int32, (tm, tm), 0)
    col = lax.broadcasted_iota(jnp.int32, (tm, tm), 1)
    before = jnp.where(col < row, 1.0, 0.0).astype(BF16)
    rank = jnp.dot(before, sel.astype(BF16), preferred_element_type=F32)
    count = jnp.sum(sel, axis=0, keepdims=True)
    tiles = jnp.floor((count + (BF16_ROWS - 1)) * (1.0 / BF16_ROWS))
    er = lax.broadcasted_iota(jnp.int32, (LANES, LANES), 0)
    ec = lax.broadcasted_iota(jnp.int32, (LANES, LANES), 1)
    lower = jnp.where(er < ec, 1.0, 0.0).astype(BF16)
    tiles8 = jnp.broadcast_to(tiles, (SUBLANES, LANES)).astype(BF16)
    start = jnp.dot(tiles8, lower, preferred_element_type=F32)[0:1] * BF16_ROWS
    slot = start + rank
    lp1 = jnp.sum(jnp.where(lane == i1, slot, 0.0), axis=-1, keepdims=True)
    lp2 = jnp.sum(jnp.where(lane == i2, slot, 0.0), axis=-1, keepdims=True)
    w_ref[0] = jnp.where(lane == 0, w1, jnp.where(lane == 1, w2, 0.0))
    lp_ref[0] = jnp.where(lane == 0, lp1, jnp.where(lane == 1, lp2, 0.0))
    sub = lax.broadcasted_iota(jnp.int32, (SUBLANES, LANES), 0)
    seg = jnp.where(sub == 0, tiles * BF16_ROWS, jnp.where(sub == 1, start, 0.0))
    seg_ref[0] = seg.astype(jnp.int32)


def _router(x, mod, g, wr, br, tm):
    b, t, d = x.shape
    nt = t // tm
    const2 = lambda i, j: (0, 0)
    tok = lambda i, j: (i, j, 0)
    return pl.pallas_call(
        _router_kernel,
        grid=(b, nt),
        in_specs=[
            pl.BlockSpec((1, tm, d), tok),
            pl.BlockSpec((1, 1, 3 * d), lambda i, j: (i, 0, 0)),
            pl.BlockSpec((1, d), const2),
            pl.BlockSpec((d, LANES), const2),
            pl.BlockSpec((1, LANES), const2),
        ],
        out_specs=[
            pl.BlockSpec((1, tm, d), tok),
            pl.BlockSpec((1, tm, LANES), tok),
            pl.BlockSpec((1, tm, LANES), tok),
            pl.BlockSpec((1, SUBLANES, LANES), lambda i, j: (i * nt + j, 0, 0)),
        ],
        out_shape=[
            jax.ShapeDtypeStruct((b, t, d), BF16),
            jax.ShapeDtypeStruct((b, t, LANES), F32),
            jax.ShapeDtypeStruct((b, t, LANES), F32),
            jax.ShapeDtypeStruct((b * nt, SUBLANES, LANES), jnp.int32),
        ],
        compiler_params=_params("parallel", "parallel"),
        name="router",
    )(x, mod, g, wr, br)


def _segment_copies(tile, dst_ref, n_ref, ls_ref, make_copy, act):
    for e in range(N_EXPERTS):
        n = n_ref[tile * N_EXPERTS + e]
        ls = ls_ref[tile * N_EXPERTS + e]
        dst = dst_ref[tile * N_EXPERTS + e]
        size = BF16_ROWS
        while size <= TOKEN_TILE:
            done = n & (-2 * size)

            @pl.when((n & size) != 0)
            def _(size=size, done=done, ls=ls, dst=dst):
                act(make_copy(pl.multiple_of(ls + done, BF16_ROWS), pl.multiple_of(dst + done, BF16_ROWS), size))

            size *= 2


def _dispatch_kernel(dst_ref, n_ref, ls_ref, h_ref, lp_ref, xs_ref, grp_s, zero_s, sem):
    tile = pl.program_id(0)
    n_tiles = pl.num_programs(0)
    tm = h_ref.shape[0]

    @pl.when(tile == 0)
    def _():
        zero_s[...] = jnp.zeros_like(zero_s)

        def make_fill(local, glob, size):
            return pltpu.make_async_copy(zero_s.at[pl.ds(local, size), :], xs_ref.at[pl.ds(glob, size), :], sem)

        _segment_copies(n_tiles, dst_ref, n_ref, ls_ref, make_fill, lambda c: c.start())
        _segment_copies(n_tiles, dst_ref, n_ref, ls_ref, make_fill, lambda c: c.wait())

        first = dst_ref[(n_tiles + 1) * N_EXPERTS]
        count = n_ref[(n_tiles + 1) * N_EXPERTS]

        def fill_tile(k):
            return make_fill(0, pl.multiple_of(first + k * tm, tm), tm)

        lax.fori_loop(0, count, lambda k, c: (fill_tile(k).start(), c)[1], 0)
        lax.fori_loop(0, count, lambda k, c: (fill_tile(k).wait(), c)[1], 0)

    gb = grp_s.shape[0]
    lpt = lp_ref[...].T
    slot = lax.broadcasted_iota(jnp.int32, (gb, tm), 0).astype(F32)
    onehot = jnp.where(slot == lpt[0:1, :], 1.0, jnp.where(slot == lpt[1:2, :], 1.0, 0.0)).astype(BF16)
    grp_s[...] = jnp.dot(onehot, h_ref[...], preferred_element_type=F32).astype(BF16)

    def make_copy(local, glob, size):
        return pltpu.make_async_copy(grp_s.at[pl.ds(local, size), :], xs_ref.at[pl.ds(glob, size), :], sem)

    _segment_copies(tile, dst_ref, n_ref, ls_ref, make_copy, lambda c: c.start())
    _segment_copies(tile, dst_ref, n_ref, ls_ref, make_copy, lambda c: c.wait())


def _dispatch(h, lp, dst, seg_n, seg_ls, slots, tm):
    n, d = h.shape
    assert tm == TOKEN_TILE
    grid_spec = pltpu.PrefetchScalarGridSpec(
        num_scalar_prefetch=3,
        grid=(n // tm,),
        in_specs=[
            pl.BlockSpec((tm, d), lambda i, *_: (i, 0)),
            pl.BlockSpec((tm, LANES), lambda i, *_: (i, 0)),
        ],
        out_specs=pl.BlockSpec(memory_space=pl.ANY),
        scratch_shapes=[pltpu.VMEM((_group_rows(tm), d), BF16), pltpu.VMEM((tm, d), BF16), pltpu.SemaphoreType.DMA(())],
    )
    return pl.pallas_call(
        _dispatch_kernel,
        grid_spec=grid_spec,
        out_shape=jax.ShapeDtypeStruct((slots, d), BF16),
        compiler_params=_params("arbitrary"),
        name="moe_dispatch",
    )(dst, seg_n, seg_ls, h, lp)


def _experts_kernel(te_ref, tv_ref, ts_ref, xs_ref, wg_ref, wu_ref, wd_ref, ys_ref, acc_s):
    del te_ref, ts_ref
    i = pl.program_id(0)
    f = pl.program_id(1)
    last = pl.num_programs(1) - 1
    live = tv_ref[i] > 0

    @pl.when(jnp.logical_and(live, f == 0))
    def _():
        acc_s[...] = jnp.zeros_like(acc_s)

    @pl.when(live)
    def _():
        h = xs_ref[...]
        a = jnp.dot(h, wg_ref[0], preferred_element_type=F32)
        u = jnp.dot(h, wu_ref[0], preferred_element_type=F32)
        acc_s[...] += jnp.dot((_silu(a) * u).astype(BF16), wd_ref[0], preferred_element_type=F32)

    @pl.when(jnp.logical_and(live, f == last))
    def _():
        ys_ref[...] = acc_s[...].astype(BF16)

    @pl.when(jnp.logical_and(jnp.logical_not(live), f == last))
    def _():
        ys_ref[...] = jnp.zeros_like(ys_ref)


def _experts(xs, tile_expert, tile_live, tile_src, wg, wu, wd, tm, tf):
    slots, d = xs.shape
    ff = wg.shape[2]
    nf = ff // tf
    fidx = lambda f, tv, i: f * tv[i] + (nf - 1) * (1 - tv[i])
    grid_spec = pltpu.PrefetchScalarGridSpec(
        num_scalar_prefetch=3,
        grid=(slots // tm, nf),
        in_specs=[
            pl.BlockSpec((tm, d), lambda i, f, te, tv, ts: (ts[i], 0)),
            pl.BlockSpec((1, d, tf), lambda i, f, te, tv, ts: (te[i], 0, fidx(f, tv, i))),
            pl.BlockSpec((1, d, tf), lambda i, f, te, tv, ts: (te[i], 0, fidx(f, tv, i))),
            pl.BlockSpec((1, tf, d), lambda i, f, te, tv, ts: (te[i], fidx(f, tv, i), 0)),
        ],
        out_specs=pl.BlockSpec((tm, d), lambda i, f, te, tv, ts: (i, 0)),
        scratch_shapes=[pltpu.VMEM((tm, d), F32)],
    )
    return pl.pallas_call(
        _experts_kernel,
        grid_spec=grid_spec,
        out_shape=jax.ShapeDtypeStruct((slots, d), BF16),
        compiler_params=_params("parallel", "arbitrary"),
        name="moe_experts",
    )(tile_expert, tile_live, tile_src, xs, wg, wu, wd)


def _combine_kernel(dst_ref, n_ref, ls_ref, x_ref, w_ref, lp_ref, mod_ref, gf_ref, ys_ref, out_ref, y_s, sem):
    d = x_ref.shape[-1]
    tm = x_ref.shape[1]
    gb = y_s.shape[0]
    tile = pl.program_id(0) * pl.num_programs(1) + pl.program_id(1)
    y_s[TOP_K * tm:, :] = jnp.zeros((gb - TOP_K * tm, d), BF16)

    def make_copy(local, glob, size):
        return pltpu.make_async_copy(ys_ref.at[pl.ds(glob, size), :], y_s.at[pl.ds(local, size), :], sem)

    _segment_copies(tile, dst_ref, n_ref, ls_ref, make_copy, lambda c: c.start())
    _segment_copies(tile, dst_ref, n_ref, ls_ref, make_copy, lambda c: c.wait())

    y = y_s[...]
    lp = lp_ref[0]
    w = w_ref[0]
    slot = lax.broadcasted_iota(jnp.int32, (tm, gb), 1).astype(F32)
    y1 = jnp.dot(jnp.where(slot == lp[:, 0:1], 1.0, 0.0).astype(BF16), y, preferred_element_type=F32)
    y2 = jnp.dot(jnp.where(slot == lp[:, 1:2], 1.0, 0.0).astype(BF16), y, preferred_element_type=F32)
    f = w[:, 0:1] * y1 + w[:, 1:2] * y2
    x = x_ref[0] + mod_ref[0, :, 2 * d:3 * d] * f
    var = jnp.mean(x * x, axis=-1, keepdims=True)
    out_ref[0] = (x * lax.rsqrt(var + RMS_EPS)) * gf_ref[...]


def _combine(x, w, lp, mod, gf, ys, dst, seg_n, seg_ls, tm):
    b, t, d = x.shape
    assert tm == TOKEN_TILE
    tok = lambda i, j, *_: (i, j, 0)
    grid_spec = pltpu.PrefetchScalarGridSpec(
        num_scalar_prefetch=3,
        grid=(b, t // tm),
        in_specs=[
            pl.BlockSpec((1, tm, d), tok),
            pl.BlockSpec((1, tm, LANES), tok),
            pl.BlockSpec((1, tm, LANES), tok),
            pl.BlockSpec((1, 1, 3 * d), lambda i, j, *_: (i, 0, 0)),
            pl.BlockSpec((1, d), lambda i, j, *_: (0, 0)),
            pl.BlockSpec(memory_space=pl.ANY),
        ],
        out_specs=pl.BlockSpec((1, tm, d), tok),
        scratch_shapes=[pltpu.VMEM((_group_rows(tm), d), BF16), pltpu.SemaphoreType.DMA(())],
    )
    return pl.pallas_call(
        _combine_kernel,
        grid_spec=grid_spec,
        out_shape=jax.ShapeDtypeStruct((b, t, d), F32),
        compiler_params=_params("arbitrary", "arbitrary"),
        name="moe_combine",
    )(dst, seg_n, seg_ls, x, w, lp, mod, gf, ys)


def _trunk(x, mods_mix, mods_ffn, p):
    b, t, d = x.shape
    tm = TOKEN_TILE
    assert t % tm == 0
    ff = p["ffn_wg"].shape[1]

    q, k, vt = _qkv(x, mods_mix[0], p["ln_mix_g"][0], p["wq"], p["wk"], p["wvt"], tm)
    rows = t // GRID_W
    rpp = 8 if (rows // 2) % 8 == 0 else 1
    o = _na(q, k, vt, p["na_bias"], head_groups=2, row_pairs_per_step=rpp)
    x = _proj_ffn(x, o, mods_mix[0], mods_ffn[0], p["ln_ffn_g"][0], p["wo"], p["ffn_wg"], p["ffn_wu"], p["ffn_wd"],
                  tm, _largest_tile(ff, FF_TILE_DENSE))

    x = _pool(x, mods_mix[1], p["ln_mix_g"][1], p["pool_w_in"], p["pool_w_grp"], p["pool_scale"], p["pool_w_out"], tm)
    h, w, lp, seg = _router(x, mods_ffn[1], p["ln_ffn_g"][1], p["wr"], p["br"], tm)

    n = b * t
    nt = n // tm
    seg_n = seg[:, 0, :N_EXPERTS]
    seg_ls = seg[:, 1, :N_EXPERTS]
    before = jnp.cumsum(seg_n, axis=0) - seg_n
    total = jnp.sum(seg_n, axis=0)
    group = ((total + tm - 1) // tm) * tm
    ends = jnp.cumsum(group)
    dst = (ends - group)[None, :] + before
    n_tiles = -(-(TOP_K * n + nt * N_EXPERTS * (BF16_ROWS - 1)) // tm) + N_EXPERTS
    tile_start = jnp.arange(n_tiles, dtype=jnp.int32) * tm
    tile_live = (tile_start < ends[-1]).astype(jnp.int32)
    tile_expert = jnp.sum((ends[None, :] <= tile_start[:, None]).astype(jnp.int32), axis=1)
    last_expert = jnp.sum((ends <= ends[-1] - 1).astype(jnp.int32))
    tile_expert = jnp.minimum(jnp.where(tile_live > 0, tile_expert, last_expert), N_EXPERTS - 1)
    tile_src = jnp.minimum(jnp.arange(n_tiles, dtype=jnp.int32), jnp.sum(tile_live) - 1)
    lead = jnp.arange(N_EXPERTS) == 0
    dst = jnp.concatenate([dst, (ends - group + total)[None, :], jnp.where(lead, ends[-1], 0)[None, :]], axis=0)
    seg_n = jnp.concatenate([seg_n, (group - total)[None, :], jnp.where(lead, n_tiles - jnp.sum(tile_live), 0)[None, :]], axis=0)
    seg_ls = jnp.concatenate([seg_ls, jnp.zeros((2, N_EXPERTS), seg_ls.dtype)], axis=0)
    dst, seg_n, seg_ls = (a.reshape(-1).astype(jnp.int32) for a in (dst, seg_n, seg_ls))

    xs = _dispatch(h.reshape(n, d), lp.reshape(n, LANES), dst, seg_n, seg_ls, n_tiles * tm, tm)
    ys = _experts(xs, tile_expert, tile_live, tile_src, p["moe_wg"], p["moe_wu"], p["moe_wd"], tm,
                  _largest_tile(ff, FF_TILE_MOE))
    return _combine(x, w, lp, mods_ffn[1], p["ln_f_g"], ys, dst, seg_n, seg_ls, tm)


def kernel(x_prompt, x_sample, c_prompt, c_sample, ln_mix_g, ada_mix_w, ada_mix_b, ln_ffn_g, ada_ffn_w, ada_ffn_b, na_w_qkv, na_rpb, na_w_o, pool_w_in, pool_w_grp, pool_scale, pool_w_out, ffn_w_gate, ffn_w_up, ffn_w_down, moe_w_router, moe_b_router, moe_w_gate, moe_w_up, moe_w_down, ln_f_g):
    d = x_prompt.shape[-1]
    depth = ln_mix_g.shape[0]
    assert depth == 2 and d == NA_HEADS * HEAD_DIM
    bp, bs = c_prompt.shape[0], c_sample.shape[0]

    pad = (-(bp + bs)) % SUBLANES
    c_all = jnp.concatenate([c_prompt, c_sample, jnp.zeros((pad, d), F32)], axis=0)
    mods_mix = _adaln(c_all, ada_mix_w, ada_mix_b)
    mods_ffn = _adaln(c_all, ada_ffn_w, ada_ffn_b)

    wqkv = na_w_qkv[0]
    ne = moe_w_router.shape[-1]
    assert ne == N_EXPERTS
    p = {
        "ln_mix_g": ln_mix_g.reshape(depth, 1, d),
        "ln_ffn_g": ln_ffn_g.reshape(depth, 1, d),
        "ln_f_g": ln_f_g.reshape(1, d),
        "wq": wqkv[:, 0:d].astype(BF16),
        "wk": wqkv[:, d:2 * d].astype(BF16),
        "wvt": wqkv[:, 2 * d:3 * d].T.astype(BF16),
        "na_bias": _na_bias_table(na_rpb[0]),
        "wo": na_w_o[0].astype(BF16),
        "ffn_wg": ffn_w_gate[0].astype(BF16),
        "ffn_wu": ffn_w_up[0].astype(BF16),
        "ffn_wd": ffn_w_down[0].astype(BF16),
        "pool_w_in": pool_w_in[0].astype(BF16),
        "pool_w_grp": pool_w_grp[0].astype(BF16),
        "pool_scale": pool_scale[0].reshape(1, d),
        "pool_w_out": pool_w_out[0].astype(BF16),
        "wr": jnp.pad(moe_w_router[0], ((0, 0), (0, LANES - ne))).astype(BF16),
        "br": jnp.pad(moe_b_router[0].astype(F32), (0, LANES - ne), constant_values=MASK_BIAS).reshape(1, LANES),
        "moe_wg": moe_w_gate[0].astype(BF16),
        "moe_wu": moe_w_up[0].astype(BF16),
        "moe_wd": moe_w_down[0].astype(BF16),
    }

    def group_mods(m, lo, n):
        return m[:, lo:lo + n, None, :]

    y_prompt = _trunk(x_prompt, group_mods(mods_mix, 0, bp), group_mods(mods_ffn, 0, bp), p)
    y_sample = _trunk(x_sample, group_mods(mods_mix, bp, bs), group_mods(mods_ffn, bp, bs), p)
    return (y_prompt, y_sample)
```

```python
import functools

import jax
import jax.numpy as jnp
from jax import lax
from jax.experimental import pallas as pl
from jax.experimental.pallas import tpu as pltpu

F32 = jnp.float32
BF16 = jnp.bfloat16

RMS_EPS = 1e-6
GRID_W = 64
NA_HEADS = 16
HEAD_DIM = 64
NA_KR = 8
NA_KW = 16
POOL_HALF = (1, 2, 4, 8)
N_EXPERTS = 8
TOP_K = 2
LANES = 128
SUBLANES = 8
BF16_ROWS = 16
MASK_BIAS = -1e30
LOG2_E = 1.4426950408889634
ROW_PAIR = 2 * GRID_W
WIN_ROWS = NA_KR + 2
BIAS_TILES = 2 * NA_KR
VMEM_LIMIT = 56 * 1024 * 1024
TOKEN_TILE = 512
FF_TILE_DENSE = 1792
FF_TILE_MOE = 1792


def _silu(a):
    return a * jax.nn.sigmoid(a)


def _mod_rmsnorm(x, g, shift, scale):
    var = jnp.mean(x * x, axis=-1, keepdims=True)
    y = (x * lax.rsqrt(var + RMS_EPS)) * g
    return y * (1.0 + scale) + shift


def _params(*sem, vmem=VMEM_LIMIT):
    return pltpu.CompilerParams(dimension_semantics=sem, vmem_limit_bytes=vmem)


def _largest_tile(total, target):
    best = LANES
    for cand in range(LANES, min(total, target) + 1, LANES):
        if total % cand == 0:
            best = cand
    assert total % best == 0
    return best


def _adaln_kernel(c_ref, w_ref, b_ref, o_ref):
    s = _silu(c_ref[...]).astype(BF16)
    o_ref[0] = jnp.dot(s, w_ref[0].astype(BF16), preferred_element_type=F32) + b_ref[0]


def _adaln(c_all, w, b):
    depth, d, d3 = w.shape
    r = c_all.shape[0]
    tn = _largest_tile(d3, 1024)
    return pl.pallas_call(
        _adaln_kernel,
        grid=(depth, d3 // tn),
        in_specs=[
            pl.BlockSpec((r, d), lambda i, j: (0, 0)),
            pl.BlockSpec((1, d, tn), lambda i, j: (i, 0, j)),
            pl.BlockSpec((1, 1, tn), lambda i, j: (i, 0, j)),
        ],
        out_specs=pl.BlockSpec((1, r, tn), lambda i, j: (i, 0, j)),
        out_shape=jax.ShapeDtypeStruct((depth, r, d3), F32),
        compiler_params=_params("parallel", "parallel"),
        name="adaln",
    )(c_all, w, b.reshape(depth, 1, d3))


def _qkv_kernel(x_ref, mod_ref, g_ref, wq_ref, wk_ref, wvt_ref, q_ref, k_ref, vt_ref):
    d = x_ref.shape[-1]
    h = _mod_rmsnorm(x_ref[0], g_ref[...], mod_ref[0, :, 0:d], mod_ref[0, :, d:2 * d]).astype(BF16)
    q = jnp.dot(h, wq_ref[...], preferred_element_type=F32) * (HEAD_DIM ** -0.5 * LOG2_E)
    q_ref[0] = q.astype(BF16)
    k_ref[0] = jnp.dot(h, wk_ref[...], preferred_element_type=F32).astype(BF16)
    vt = lax.dot_general(wvt_ref[...], h, (((1,), (1,)), ((), ())), preferred_element_type=F32)
    for j in range(vt_ref.shape[1]):
        vt_ref[0, j] = vt[:, j * LANES:(j + 1) * LANES].astype(BF16)


def _qkv(x, mod, g, wq, wk, wvt, tm):
    b, t, d = x.shape
    const = lambda i, j: (0, 0)
    return pl.pallas_call(
        _qkv_kernel,
        grid=(b, t // tm),
        in_specs=[
            pl.BlockSpec((1, tm, d), lambda i, j: (i, j, 0)),
            pl.BlockSpec((1, 1, 3 * d), lambda i, j: (i, 0, 0)),
            pl.BlockSpec((1, d), const),
            pl.BlockSpec((d, d), const),
            pl.BlockSpec((d, d), const),
            pl.BlockSpec((d, d), const),
        ],
        out_specs=[
            pl.BlockSpec((1, tm, d), lambda i, j: (i, j, 0)),
            pl.BlockSpec((1, tm, d), lambda i, j: (i, j, 0)),
            pl.BlockSpec((1, tm // LANES, d, LANES), lambda i, j: (i, j, 0, 0)),
        ],
        out_shape=[
            jax.ShapeDtypeStruct((b, t, d), BF16),
            jax.ShapeDtypeStruct((b, t, d), BF16),
            jax.ShapeDtypeStruct((b, t // LANES, d, LANES), BF16),
        ],
        compiler_params=_params("parallel", "parallel"),
        name="qkv",
    )(x, mod, g, wq, wk, wvt)


def _na_bias_table(rpb):
    h = rpb.shape[0]
    c = jnp.arange(GRID_W, dtype=jnp.int32)
    cs = jnp.clip(c - NA_KW // 2, 0, GRID_W - NA_KW)
    kc = c[:, None]
    valid = (kc >= cs[None, :]) & (kc < cs[None, :] + NA_KW)
    dc = jnp.clip(kc - c[None, :] + (NA_KW - 1), 0, 2 * NA_KW - 2)
    bt = jnp.where(valid, rpb.astype(F32)[:, :, dc], MASK_BIAS)
    bt = jnp.concatenate([bt, jnp.full((h, 1, GRID_W, GRID_W), MASK_BIAS, F32)], axis=1)
    bt = bt.reshape(h // 2, 2, BIAS_TILES, GRID_W, GRID_W).transpose(0, 2, 3, 1, 4)
    return bt.reshape(h // 2, BIAS_TILES, GRID_W, 2 * GRID_W) * LOG2_E


def _na_kernel(q_ref, k_ref, vt_ref, bias_ref, o_ref, *, rows, pairs_per_step, row_pairs_per_step):
    step = pl.program_id(2)
    lane = lax.broadcasted_iota(jnp.int32, (ROW_PAIR, LANES), 1)
    first_head = lane < HEAD_DIM
    first_head_row = lax.broadcasted_iota(jnp.int32, (GRID_W, LANES), 1) < HEAD_DIM

    def row_pair(j, carry):
        rp = step * row_pairs_per_step + j
        r0 = 2 * rp
        wblk = jnp.clip(rp - NA_KR // 4, 0, (rows - WIN_ROWS) // 2)
        ws = 2 * wblk
        tok = j * ROW_PAIR
        ktok = pl.multiple_of(ws * GRID_W, ROW_PAIR)
        for p in range(pairs_per_step):
            cols = slice(p * LANES, (p + 1) * LANES)
            q2 = q_ref[0, pl.ds(tok, ROW_PAIR), cols]
            qa = jnp.where(first_head, q2, jnp.zeros_like(q2))
            qb = jnp.where(first_head, jnp.zeros_like(q2), q2)
            qbd = jnp.concatenate([qa[:GRID_W], qb[:GRID_W], qa[GRID_W:], qb[GRID_W:]], axis=0)
            kwin = k_ref[0, pl.ds(ktok, WIN_ROWS * GRID_W), cols]
            st = lax.dot_general(kwin, qbd, (((1,), (1,)), ((), ())), preferred_element_type=F32)
            blocks = []
            for i in range(WIN_ROWS):
                kr = ws + i
                tiles = []
                for s in range(2):
                    r = r0 + s
                    rs = jnp.clip(r - NA_KR // 2, 0, rows - NA_KR)
                    inside = jnp.logical_and(kr >= rs, kr < rs + NA_KR)
                    idx = jnp.where(inside, kr - r + (NA_KR - 1), BIAS_TILES - 1)
                    tiles.append(bias_ref[p, idx])
                blocks.append(st[i * GRID_W:(i + 1) * GRID_W] + jnp.concatenate(tiles, axis=1))
            sc = jnp.concatenate(blocks, axis=0)
            m = jnp.max(sc, axis=0, keepdims=True)
            e = jnp.exp2(sc - m)
            l = jnp.sum(e, axis=0, keepdims=True)
            pt = e.astype(BF16)
            ot = jnp.zeros((LANES, 2 * LANES), F32)
            for c in range(WIN_ROWS * GRID_W // LANES):
                vt = vt_ref[0, wblk + c, cols, :]
                ot = ot + jnp.dot(vt, pt[c * LANES:(c + 1) * LANES], preferred_element_type=F32)
            ot = ot * (1.0 / l)
            outs = []
            for s in range(2):
                tt = ot[:, s * LANES:(s + 1) * LANES].T
                outs.append(jnp.where(first_head_row, tt[:GRID_W], tt[GRID_W:]))
            o_ref[0, pl.ds(tok, ROW_PAIR), cols] = jnp.concatenate(outs, axis=0).astype(BF16)
        return carry

    for j in range(row_pairs_per_step):
        row_pair(j, 0)


def _na(q, k, vt, bias, *, head_groups, row_pairs_per_step):
    b, t, d = q.shape
    rows = t // GRID_W
    assert rows % 2 == 0 and rows >= WIN_ROWS and (rows // 2) % row_pairs_per_step == 0
    gc = d // head_groups
    pairs = gc // LANES
    tq = row_pairs_per_step * ROW_PAIR
    kern = functools.partial(_na_kernel, rows=rows, pairs_per_step=pairs, row_pairs_per_step=row_pairs_per_step)
    return pl.pallas_call(
        kern,
        grid=(b, head_groups, t // tq),
        in_specs=[
            pl.BlockSpec((1, tq, gc), lambda i, g, j: (i, j, g)),
            pl.BlockSpec((1, t, gc), lambda i, g, j: (i, 0, g)),
            pl.BlockSpec((1, t // LANES, gc, LANES), lambda i, g, j: (i, 0, g, 0)),
            pl.BlockSpec((pairs, BIAS_TILES, GRID_W, LANES), lambda i, g, j: (g, 0, 0, 0)),
        ],
        out_specs=pl.BlockSpec((1, tq, gc), lambda i, g, j: (i, j, g)),
        out_shape=jax.ShapeDtypeStruct((b, t, d), BF16),
        compiler_params=_params("parallel", "parallel", "parallel"),
        name="na_attn",
    )(q, k, vt, bias)


def _proj_ffn_kernel(x_ref, o_ref, mm_ref, mf_ref, g_ref, wo_ref, wg_ref, wu_ref, wd_ref, out_ref, x1_s, h_s, acc_s):
    d = x_ref.shape[-1]
    f = pl.program_id(2)

    @pl.when(f == 0)
    def _():
        m = jnp.dot(o_ref[0], wo_ref[...], preferred_element_type=F32)
        x1 = x_ref[0] + mm_ref[0, :, 2 * d:3 * d] * m
        x1_s[...] = x1
        h_s[...] = _mod_rmsnorm(x1, g_ref[...], mf_ref[0, :, 0:d], mf_ref[0, :, d:2 * d]).astype(BF16)
        acc_s[...] = jnp.zeros_like(acc_s)

    h = h_s[...]
    a = jnp.dot(h, wg_ref[...], preferred_element_type=F32)
    u = jnp.dot(h, wu_ref[...], preferred_element_type=F32)
    acc_s[...] += jnp.dot((_silu(a) * u).astype(BF16), wd_ref[...], preferred_element_type=F32)

    @pl.when(f == pl.num_programs(2) - 1)
    def _():
        out_ref[0] = x1_s[...] + mf_ref[0, :, 2 * d:3 * d] * acc_s[...]


def _proj_ffn(x, o, mod_mix, mod_ffn, g, wo, wg, wu, wd, tm, tf):
    b, t, d = x.shape
    ff = wg.shape[1]
    tok = lambda i, j, f: (i, j, 0)
    seq = lambda i, j, f: (i, 0, 0)
    return pl.pallas_call(
        _proj_ffn_kernel,
        grid=(b, t // tm, ff // tf),
        in_specs=[
            pl.BlockSpec((1, tm, d), tok),
            pl.BlockSpec((1, tm, d), tok),
            pl.BlockSpec((1, 1, 3 * d), seq),
            pl.BlockSpec((1, 1, 3 * d), seq),
            pl.BlockSpec((1, d), lambda i, j, f: (0, 0)),
            pl.BlockSpec((d, d), lambda i, j, f: (0, 0)),
            pl.BlockSpec((d, tf), lambda i, j, f: (0, f)),
            pl.BlockSpec((d, tf), lambda i, j, f: (0, f)),
            pl.BlockSpec((tf, d), lambda i, j, f: (f, 0)),
        ],
        out_specs=pl.BlockSpec((1, tm, d), tok),
        out_shape=jax.ShapeDtypeStruct((b, t, d), F32),
        scratch_shapes=[pltpu.VMEM((tm, d), F32), pltpu.VMEM((tm, d), BF16), pltpu.VMEM((tm, d), F32)],
        compiler_params=_params("parallel", "parallel", "arbitrary"),
        name="proj_ffn",
    )(x, o, mod_mix, mod_ffn, g, wo, wg, wu, wd)


def _pool_route_kernel(x_ref, xp_ref, xn_ref, mod_ref, g_ref, win_ref, wgrp_ref, ps_ref, wout_ref,
                       modf_ref, gf_ref, wr_ref, br_ref, out_ref, h_ref, w_ref, lp_ref, seg_ref, *, seq_len):
    d = x_ref.shape[-1]
    tm = x_ref.shape[1]
    halo = SUBLANES
    n = tm + 2 * halo
    j = pl.program_id(1)
    x = x_ref[0]
    xe = jnp.concatenate([xp_ref[0], x, xn_ref[0]], axis=0)
    h = _mod_rmsnorm(xe, g_ref[...], mod_ref[0, :, 0:d], mod_ref[0, :, d:2 * d]).astype(BF16)
    u = jnp.dot(h, win_ref[...], preferred_element_type=F32)
    pos = j * tm - halo + lax.broadcasted_iota(jnp.int32, (n, 1), 0)
    u = jnp.where(jnp.logical_and(pos >= 0, pos < seq_len), u, 0.0)
    gc = d // len(POOL_HALF)
    ys = []
    for gi, half in enumerate(POOL_HALF):
        ug = u[:, gi * gc:(gi + 1) * gc]
        win = ug + pltpu.roll(ug, 1, axis=0)
        span = 1
        while span < half:
            win = pltpu.roll(win, span, axis=0) + pltpu.roll(win, n - span, axis=0)
            span *= 2
        cnt = jnp.clip(pos + half, 0, seq_len) - jnp.clip(pos - half, 0, seq_len)
        cnt = jnp.maximum(cnt, 1).astype(F32)
        pooled = (win / cnt - ug)[halo:halo + tm].astype(BF16)
        y = jnp.dot(pooled, wgrp_ref[gi], preferred_element_type=F32) * ps_ref[:, gi * gc:(gi + 1) * gc]
        ys.append(y.astype(BF16))
    m = jnp.dot(jnp.concatenate(ys, axis=1), wout_ref[...], preferred_element_type=F32)
    xo = x + mod_ref[0, :, 2 * d:3 * d] * m
    out_ref[0] = xo
    _route(xo, modf_ref, gf_ref, wr_ref, br_ref, h_ref, w_ref, lp_ref, seg_ref)


def _pool_route(x, mod, g, w_in, w_grp, pscale, w_out, mod_ffn, g_ffn, wr, br, tm):
    b, t, d = x.shape
    nt = t // tm
    hb = tm // SUBLANES
    last = t // SUBLANES - 1
    const2 = lambda i, j: (0, 0)
    tok = lambda i, j: (i, j, 0)
    return pl.pallas_call(
        functools.partial(_pool_route_kernel, seq_len=t),
        grid=(b, nt),
        in_specs=[
            pl.BlockSpec((1, tm, d), lambda i, j: (i, j, 0)),
            pl.BlockSpec((1, SUBLANES, d), lambda i, j: (i, jnp.maximum(j * hb - 1, 0), 0)),
            pl.BlockSpec((1, SUBLANES, d), lambda i, j: (i, jnp.minimum((j + 1) * hb, last), 0)),
            pl.BlockSpec((1, 1, 3 * d), lambda i, j: (i, 0, 0)),
            pl.BlockSpec((1, d), const2),
            pl.BlockSpec((d, d), const2),
            pl.BlockSpec(w_grp.shape, lambda i, j: (0, 0, 0)),
            pl.BlockSpec((1, d), const2),
            pl.BlockSpec((d, d), const2),
            pl.BlockSpec((1, 1, 3 * d), lambda i, j: (i, 0, 0)),
            pl.BlockSpec((1, d), const2),
            pl.BlockSpec((d, LANES), const2),
            pl.BlockSpec((1, LANES), const2),
        ],
        out_specs=[
            pl.BlockSpec((1, tm, d), tok),
            pl.BlockSpec((1, tm, d), tok),
            pl.BlockSpec((1, tm, LANES), tok),
            pl.BlockSpec((1, tm, LANES), tok),
            pl.BlockSpec((1, SUBLANES, LANES), lambda i, j: (i * nt + j, 0, 0)),
        ],
        out_shape=[
            jax.ShapeDtypeStruct((b, t, d), F32),
            jax.ShapeDtypeStruct((b, t, d), BF16),
            jax.ShapeDtypeStruct((b, t, LANES), F32),
            jax.ShapeDtypeStruct((b, t, LANES), F32),
            jax.ShapeDtypeStruct((b * nt, SUBLANES, LANES), jnp.int32),
        ],
        compiler_params=_params("parallel", "parallel"),
        name="pool_route",
    )(x, x, x, mod, g, w_in, w_grp, pscale, w_out, mod_ffn, g_ffn, wr, br)


def _group_rows(tm):
    return TOP_K * tm + N_EXPERTS * BF16_ROWS


def _route(x, mod_ref, g_ref, wr_ref, br_ref, h_ref, w_ref, lp_ref, seg_ref):
    tm, d = x.shape
    h = _mod_rmsnorm(x, g_ref[...], mod_ref[0, :, 0:d], mod_ref[0, :, d:2 * d]).astype(BF16)
    h_ref[0] = h
    logits = jnp.dot(h, wr_ref[...], preferred_element_type=F32) + br_ref[...]
    lane = lax.broadcasted_iota(jnp.int32, logits.shape, 1).astype(F32)
    v1 = jnp.max(logits, axis=-1, keepdims=True)
    i1 = jnp.min(jnp.where(logits == v1, lane, float(LANES)), axis=-1, keepdims=True)
    rest = jnp.where(lane == i1, -jnp.inf, logits)
    v2 = jnp.max(rest, axis=-1, keepdims=True)
    i2 = jnp.min(jnp.where(rest == v2, lane, float(LANES)), axis=-1, keepdims=True)
    e2 = jnp.exp(v2 - v1)
    w1 = 1.0 / (1.0 + e2)
    w2 = e2 / (1.0 + e2)
    sel = jnp.where(lane == i1, 1.0, jnp.where(lane == i2, 1.0, 0.0))
    row = lax.broadcasted_iota(jnp.int32, (tm, tm), 0)
    col = lax.broadcasted_iota(jnp.int32, (tm, tm), 1)
    before = jnp.where(col < row, 1.0, 0.0).astype(BF16)
    rank = jnp.dot(before, sel.astype(BF16), preferred_element_type=F32)
    count = jnp.sum(sel, axis=0, keepdims=True)
    tiles = jnp.floor((count + (BF16_ROWS - 1)) * (1.0 / BF16_ROWS))
    er = lax.broadcasted_iota(jnp.int32, (LANES, LANES), 0)
    ec = lax.broadcasted_iota(jnp.int32, (LANES, LANES), 1)
    lower = jnp.where(er < ec, 1.0, 0.0).astype(BF16)
    tiles8 = jnp.broadcast_to(tiles, (SUBLANES, LANES)).astype(BF16)
    start = jnp.dot(tiles8, lower, preferred_element_type=F32)[0:1] * BF16_ROWS
    slot = start + rank
    lp1 = jnp.sum(jnp.where(lane == i1, slot, 0.0), axis=-1, keepdims=True)
    lp2 = jnp.sum(jnp.where(lane == i2, slot, 0.0), axis=-1, keepdims=True)
    w_ref[0] = jnp.where(lane == 0, w1, jnp.where(lane == 1, w2, 0.0))
    lp_ref[0] = jnp.where(lane == 0, lp1, jnp.where(lane == 1, lp2, 0.0))
    sub = lax.broadcasted_iota(jnp.int32, (SUBLANES, LANES), 0)
    seg = jnp.where(sub == 0, tiles * BF16_ROWS, jnp.where(sub == 1, start, 0.0))
    seg_ref[0] = seg.astype(jnp.int32)


def _segment_copies(tile, dst_ref, n_ref, ls_ref, make_copy, act):
    for e in range(N_EXPERTS):
        n = n_ref[tile * N_EXPERTS + e]
        ls = ls_ref[tile * N_EXPERTS + e]
        dst = dst_ref[tile * N_EXPERTS + e]
        size = BF16_ROWS
        while size <= TOKEN_TILE:
            done = n & (-2 * size)

            @pl.when((n & size) != 0)
            def _(size=size, done=done, ls=ls, dst=dst):
                act(make_copy(pl.multiple_of(ls + done, BF16_ROWS), pl.multiple_of(dst + done, BF16_ROWS), size))

            size *= 2


def _dispatch_kernel(dst_ref, n_ref, ls_ref, h_ref, lp_ref, xs_ref, grp_s, zero_s, sem, seg_sem):
    tile = pl.program_id(0)
    n_tiles = pl.num_programs(0)
    tm = h_ref.shape[0]

    @pl.when(tile == 0)
    def _():
        zero_s[...] = jnp.zeros_like(zero_s)

        def make_fill(local, glob, size):
            return pltpu.make_async_copy(zero_s.at[pl.ds(local, size), :], xs_ref.at[pl.ds(glob, size), :], sem)

        _segment_copies(n_tiles, dst_ref, n_ref, ls_ref, make_fill, lambda c: c.start())
        _segment_copies(n_tiles, dst_ref, n_ref, ls_ref, make_fill, lambda c: c.wait())

        first = dst_ref[(n_tiles + 1) * N_EXPERTS]
        count = n_ref[(n_tiles + 1) * N_EXPERTS]

        def fill_tile(k):
            return make_fill(0, pl.multiple_of(first + k * tm, tm), tm)

        lax.fori_loop(0, count, lambda k, c: (fill_tile(k).start(), c)[1], 0)
        lax.fori_loop(0, count, lambda k, c: (fill_tile(k).wait(), c)[1], 0)

    gb = grp_s.shape[1]
    buf = tile % 2
    lpt = lp_ref[...].T
    slot = lax.broadcasted_iota(jnp.int32, (gb, tm), 0).astype(F32)
    onehot = jnp.where(slot == lpt[0:1, :], 1.0, jnp.where(slot == lpt[1:2, :], 1.0, 0.0)).astype(BF16)
    grp_s[buf] = jnp.dot(onehot, h_ref[...], preferred_element_type=F32).astype(BF16)

    def copies(t, b, act):
        def make_copy(local, glob, size):
            return pltpu.make_async_copy(grp_s.at[b, pl.ds(local, size), :], xs_ref.at[pl.ds(glob, size), :], seg_sem.at[b])

        _segment_copies(t, dst_ref, n_ref, ls_ref, make_copy, act)

    copies(tile, buf, lambda c: c.start())

    @pl.when(tile > 0)
    def _():
        copies(tile - 1, 1 - buf, lambda c: c.wait())

    @pl.when(tile == n_tiles - 1)
    def _():
        copies(tile, buf, lambda c: c.wait())


def _dispatch(h, lp, dst, seg_n, seg_ls, slots, tm):
    n, d = h.shape
    assert tm == TOKEN_TILE
    grid_spec = pltpu.PrefetchScalarGridSpec(
        num_scalar_prefetch=3,
        grid=(n // tm,),
        in_specs=[
            pl.BlockSpec((tm, d), lambda i, *_: (i, 0)),
            pl.BlockSpec((tm, LANES), lambda i, *_: (i, 0)),
        ],
        out_specs=pl.BlockSpec(memory_space=pl.ANY),
        scratch_shapes=[pltpu.VMEM((2, _group_rows(tm), d), BF16), pltpu.VMEM((tm, d), BF16),
                        pltpu.SemaphoreType.DMA(()), pltpu.SemaphoreType.DMA((2,))],
    )
    return pl.pallas_call(
        _dispatch_kernel,
        grid_spec=grid_spec,
        out_shape=jax.ShapeDtypeStruct((slots, d), BF16),
        compiler_params=_params("arbitrary"),
        name="moe_dispatch",
    )(dst, seg_n, seg_ls, h, lp)


def _experts_kernel(te_ref, tv_ref, ts_ref, xs_ref, wg_ref, wu_ref, wd_ref, ys_ref, acc_s):
    del te_ref, ts_ref
    i = pl.program_id(0)
    f = pl.program_id(1)
    last = pl.num_programs(1) - 1
    live = tv_ref[i] > 0

    @pl.when(jnp.logical_and(live, f == 0))
    def _():
        acc_s[...] = jnp.zeros_like(acc_s)

    @pl.when(live)
    def _():
        h = xs_ref[...]
        a = jnp.dot(h, wg_ref[0], preferred_element_type=F32)
        u = jnp.dot(h, wu_ref[0], preferred_element_type=F32)
        acc_s[...] += jnp.dot((_silu(a) * u).astype(BF16), wd_ref[0], preferred_element_type=F32)

    @pl.when(jnp.logical_and(live, f == last))
    def _():
        ys_ref[...] = acc_s[...].astype(BF16)

    @pl.when(jnp.logical_and(jnp.logical_not(live), f == last))
    def _():
        ys_ref[...] = jnp.zeros_like(ys_ref)


def _experts(xs, tile_expert, tile_live, tile_src, wg, wu, wd, tm, tf):
    slots, d = xs.shape
    ff = wg.shape[2]
    nf = ff // tf
    fidx = lambda f, tv, i: f * tv[i] + (nf - 1) * (1 - tv[i])
    grid_spec = pltpu.PrefetchScalarGridSpec(
        num_scalar_prefetch=3,
        grid=(slots // tm, nf),
        in_specs=[
            pl.BlockSpec((tm, d), lambda i, f, te, tv, ts: (ts[i], 0)),
            pl.BlockSpec((1, d, tf), lambda i, f, te, tv, ts: (te[i], 0, fidx(f, tv, i))),
            pl.BlockSpec((1, d, tf), lambda i, f, te, tv, ts: (te[i], 0, fidx(f, tv, i))),
            pl.BlockSpec((1, tf, d), lambda i, f, te, tv, ts: (te[i], fidx(f, tv, i), 0)),
        ],
        out_specs=pl.BlockSpec((tm, d), lambda i, f, te, tv, ts: (i, 0)),
        scratch_shapes=[pltpu.VMEM((tm, d), F32)],
    )
    return pl.pallas_call(
        _experts_kernel,
        grid_spec=grid_spec,
        out_shape=jax.ShapeDtypeStruct((slots, d), BF16),
        compiler_params=_params("parallel", "arbitrary"),
        name="moe_experts",
    )(tile_expert, tile_live, tile_src, xs, wg, wu, wd)


def _combine_kernel(dst_ref, n_ref, ls_ref, x_ref, w_ref, lp_ref, mod_ref, gf_ref, ys_ref, out_ref, y_s, sem):
    d = x_ref.shape[-1]
    tm = x_ref.shape[1]
    gb = y_s.shape[1]
    tile = pl.program_id(0) * pl.num_programs(1) + pl.program_id(1)
    n_tiles = pl.num_programs(0) * pl.num_programs(1)
    buf = tile % 2

    def fetch(t, b, act):
        def make_copy(local, glob, size):
            return pltpu.make_async_copy(ys_ref.at[pl.ds(glob, size), :], y_s.at[b, pl.ds(local, size), :], sem.at[b])

        _segment_copies(t, dst_ref, n_ref, ls_ref, make_copy, act)

    def start_fetch(t, b):
        y_s[b, TOP_K * tm:, :] = jnp.zeros((gb - TOP_K * tm, d), BF16)
        fetch(t, b, lambda c: c.start())

    @pl.when(tile == 0)
    def _():
        start_fetch(tile, buf)

    @pl.when(tile + 1 < n_tiles)
    def _():
        start_fetch(tile + 1, 1 - buf)

    fetch(tile, buf, lambda c: c.wait())

    y = y_s[buf]
    lp = lp_ref[0]
    w = w_ref[0]
    slot = lax.broadcasted_iota(jnp.int32, (tm, gb), 1).astype(F32)
    y1 = jnp.dot(jnp.where(slot == lp[:, 0:1], 1.0, 0.0).astype(BF16), y, preferred_element_type=F32)
    y2 = jnp.dot(jnp.where(slot == lp[:, 1:2], 1.0, 0.0).astype(BF16), y, preferred_element_type=F32)
    f = w[:, 0:1] * y1 + w[:, 1:2] * y2
    x = x_ref[0] + mod_ref[0, :, 2 * d:3 * d] * f
    var = jnp.mean(x * x, axis=-1, keepdims=True)
    out_ref[0] = (x * lax.rsqrt(var + RMS_EPS)) * gf_ref[...]


def _combine(x, w, lp, mod, gf, ys, dst, seg_n, seg_ls, tm):
    b, t, d = x.shape
    assert tm == TOKEN_TILE
    tok = lambda i, j, *_: (i, j, 0)
    grid_spec = pltpu.PrefetchScalarGridSpec(
        num_scalar_prefetch=3,
        grid=(b, t // tm),
        in_specs=[
            pl.BlockSpec((1, tm, d), tok),
            pl.BlockSpec((1, tm, LANES), tok),
            pl.BlockSpec((1, tm, LANES), tok),
            pl.BlockSpec((1, 1, 3 * d), lambda i, j, *_: (i, 0, 0)),
            pl.BlockSpec((1, d), lambda i, j, *_: (0, 0)),
            pl.BlockSpec(memory_space=pl.ANY),
        ],
        out_specs=pl.BlockSpec((1, tm, d), tok),
        scratch_shapes=[pltpu.VMEM((2, _group_rows(tm), d), BF16), pltpu.SemaphoreType.DMA((2,))],
    )
    return pl.pallas_call(
        _combine_kernel,
        grid_spec=grid_spec,
        out_shape=jax.ShapeDtypeStruct((b, t, d), F32),
        compiler_params=_params("arbitrary", "arbitrary"),
        name="moe_combine",
    )(dst, seg_n, seg_ls, x, w, lp, mod, gf, ys)


def _trunk(x, mods_mix, mods_ffn, p):
    b, t, d = x.shape
    tm = TOKEN_TILE
    assert t % tm == 0
    ff = p["ffn_wg"].shape[1]

    q, k, vt = _qkv(x, mods_mix[0], p["ln_mix_g"][0], p["wq"], p["wk"], p["wvt"], tm)
    rows = t // GRID_W
    rpp = 8 if (rows // 2) % 8 == 0 else 1
    o = _na(q, k, vt, p["na_bias"], head_groups=2, row_pairs_per_step=rpp)
    x = _proj_ffn(x, o, mods_mix[0], mods_ffn[0], p["ln_ffn_g"][0], p["wo"], p["ffn_wg"], p["ffn_wu"], p["ffn_wd"],
                  tm, _largest_tile(ff, FF_TILE_DENSE))

    x, h, w, lp, seg = _pool_route(x, mods_mix[1], p["ln_mix_g"][1], p["pool_w_in"], p["pool_w_grp"], p["pool_scale"],
                                   p["pool_w_out"], mods_ffn[1], p["ln_ffn_g"][1], p["wr"], p["br"], tm)

    n = b * t
    nt = n // tm
    seg_n = seg[:, 0, :N_EXPERTS]
    seg_ls = seg[:, 1, :N_EXPERTS]
    before = jnp.cumsum(seg_n, axis=0) - seg_n
    total = jnp.sum(seg_n, axis=0)
    group = ((total + tm - 1) // tm) * tm
    ends = jnp.cumsum(group)
    dst = (ends - group)[None, :] + before
    n_tiles = -(-(TOP_K * n + nt * N_EXPERTS * (BF16_ROWS - 1)) // tm) + N_EXPERTS
    tile_start = jnp.arange(n_tiles, dtype=jnp.int32) * tm
    tile_live = (tile_start < ends[-1]).astype(jnp.int32)
    tile_expert = jnp.sum((ends[None, :] <= tile_start[:, None]).astype(jnp.int32), axis=1)
    last_expert = jnp.sum((ends <= ends[-1] - 1).astype(jnp.int32))
    tile_expert = jnp.minimum(jnp.where(tile_live > 0, tile_expert, last_expert), N_EXPERTS - 1)
    tile_src = jnp.minimum(jnp.arange(n_tiles, dtype=jnp.int32), jnp.sum(tile_live) - 1)
    lead = jnp.arange(N_EXPERTS) == 0
    dst = jnp.concatenate([dst, (ends - group + total)[None, :], jnp.where(lead, ends[-1], 0)[None, :]], axis=0)
    seg_n = jnp.concatenate([seg_n, (group - total)[None, :], jnp.where(lead, n_tiles - jnp.sum(tile_live), 0)[None, :]], axis=0)
    seg_ls = jnp.concatenate([seg_ls, jnp.zeros((2, N_EXPERTS), seg_ls.dtype)], axis=0)
    dst, seg_n, seg_ls = (a.reshape(-1).astype(jnp.int32) for a in (dst, seg_n, seg_ls))

    xs = _dispatch(h.reshape(n, d), lp.reshape(n, LANES), dst, seg_n, seg_ls, n_tiles * tm, tm)
    ys = _experts(xs, tile_expert, tile_live, tile_src, p["moe_wg"], p["moe_wu"], p["moe_wd"], tm,
                  _largest_tile(ff, FF_TILE_MOE))
    return _combine(x, w, lp, mods_ffn[1], p["ln_f_g"], ys, dst, seg_n, seg_ls, tm)


def kernel(x_prompt, x_sample, c_prompt, c_sample, ln_mix_g, ada_mix_w, ada_mix_b, ln_ffn_g, ada_ffn_w, ada_ffn_b, na_w_qkv, na_rpb, na_w_o, pool_w_in, pool_w_grp, pool_scale, pool_w_out, ffn_w_gate, ffn_w_up, ffn_w_down, moe_w_router, moe_b_router, moe_w_gate, moe_w_up, moe_w_down, ln_f_g):
    d = x_prompt.shape[-1]
    depth = ln_mix_g.shape[0]
    assert depth == 2 and d == NA_HEADS * HEAD_DIM
    bp, bs = c_prompt.shape[0], c_sample.shape[0]

    pad = (-(bp + bs)) % SUBLANES
    c_all = jnp.concatenate([c_prompt, c_sample, jnp.zeros((pad, d), F32)], axis=0)
    mods_mix = _adaln(c_all, ada_mix_w, ada_mix_b)
    mods_ffn = _adaln(c_all, ada_ffn_w, ada_ffn_b)

    wqkv = na_w_qkv[0]
    ne = moe_w_router.shape[-1]
    assert ne == N_EXPERTS
    p = {
        "ln_mix_g": ln_mix_g.reshape(depth, 1, d),
        "ln_ffn_g": ln_ffn_g.reshape(depth, 1, d),
        "ln_f_g": ln_f_g.reshape(1, d),
        "wq": wqkv[:, 0:d].astype(BF16),
        "wk": wqkv[:, d:2 * d].astype(BF16),
        "wvt": wqkv[:, 2 * d:3 * d].T.astype(BF16),
        "na_bias": _na_bias_table(na_rpb[0]),
        "wo": na_w_o[0].astype(BF16),
        "ffn_wg": ffn_w_gate[0].astype(BF16),
        "ffn_wu": ffn_w_up[0].astype(BF16),
        "ffn_wd": ffn_w_down[0].astype(BF16),
        "pool_w_in": pool_w_in[0].astype(BF16),
        "pool_w_grp": pool_w_grp[0].astype(BF16),
        "pool_scale": pool_scale[0].reshape(1, d),
        "pool_w_out": pool_w_out[0].astype(BF16),
        "wr": jnp.pad(moe_w_router[0], ((0, 0), (0, LANES - ne))).astype(BF16),
        "br": jnp.pad(moe_b_router[0].astype(F32), (0, LANES - ne), constant_values=MASK_BIAS).reshape(1, LANES),
        "moe_wg": moe_w_gate[0].astype(BF16),
        "moe_wu": moe_w_up[0].astype(BF16),
        "moe_wd": moe_w_down[0].astype(BF16),
    }

    def group_mods(m, lo, n):
        return m[:, lo:lo + n, None, :]

    y_prompt = _trunk(x_prompt, group_mods(mods_mix, 0, bp), group_mods(mods_ffn, 0, bp), p)
    y_sample = _trunk(x_sample, group_mods(mods_mix, bp, bs), group_mods(mods_ffn, bp, bs), p)
    return (y_prompt, y_sample)
```

```python
import functools

import jax
import jax.numpy as jnp
from jax import lax
from jax.experimental import pallas as pl
from jax.experimental.pallas import tpu as pltpu

F32 = jnp.float32
BF16 = jnp.bfloat16

RMS_EPS = 1e-6
GRID_W = 64
NA_HEADS = 16
HEAD_DIM = 64
NA_KR = 8
NA_KW = 16
POOL_HALF = (1, 2, 4, 8)
N_EXPERTS = 8
TOP_K = 2
LANES = 128
SUBLANES = 8
BF16_ROWS = 16
MASK_BIAS = -1e30
LOG2_E = 1.4426950408889634
ROW_PAIR = 2 * GRID_W
WIN_ROWS = NA_KR + 2
BIAS_TILES = 2 * NA_KR
VMEM_LIMIT = 56 * 1024 * 1024
TOKEN_TILE = 512
FF_TILE_DENSE = 1792
FF_TILE_MOE = 1792
EXPERT_TILE_PARTS = 4


def _silu(a):
    return a * jax.nn.sigmoid(a)


def _mod_rmsnorm(x, g, shift, scale):
    var = jnp.mean(x * x, axis=-1, keepdims=True)
    y = (x * lax.rsqrt(var + RMS_EPS)) * g
    return y * (1.0 + scale) + shift


def _params(*sem, vmem=VMEM_LIMIT):
    return pltpu.CompilerParams(dimension_semantics=sem, vmem_limit_bytes=vmem)


def _fuse_gate_up(wg, wu, tf):
    *lead, d, ff = wg.shape
    blocks = lambda w: w.reshape(*lead, d, ff // tf, 1, tf)
    return jnp.concatenate([blocks(wg), blocks(wu)], axis=-2).astype(BF16).reshape(*lead, d, 2 * ff)


def _largest_tile(total, target):
    best = LANES
    for cand in range(LANES, min(total, target) + 1, LANES):
        if total % cand == 0:
            best = cand
    assert total % best == 0
    return best


def _adaln_kernel(c_ref, w_ref, b_ref, o_ref):
    s = _silu(c_ref[...]).astype(BF16)
    o_ref[0] = jnp.dot(s, w_ref[0].astype(BF16), preferred_element_type=F32) + b_ref[0]


def _adaln(c_all, w, b):
    depth, d, d3 = w.shape
    r = c_all.shape[0]
    tn = _largest_tile(d3, 1024)
    return pl.pallas_call(
        _adaln_kernel,
        grid=(depth, d3 // tn),
        in_specs=[
            pl.BlockSpec((r, d), lambda i, j: (0, 0)),
            pl.BlockSpec((1, d, tn), lambda i, j: (i, 0, j)),
            pl.BlockSpec((1, 1, tn), lambda i, j: (i, 0, j)),
        ],
        out_specs=pl.BlockSpec((1, r, tn), lambda i, j: (i, 0, j)),
        out_shape=jax.ShapeDtypeStruct((depth, r, d3), F32),
        compiler_params=_params("parallel", "parallel"),
        name="adaln",
    )(c_all, w, b.reshape(depth, 1, d3))


def _qkv_kernel(x_ref, mod_ref, g_ref, wq_ref, wk_ref, wvt_ref, q_ref, k_ref, vt_ref):
    d = x_ref.shape[-1]
    h = _mod_rmsnorm(x_ref[0], g_ref[...], mod_ref[0, :, 0:d], mod_ref[0, :, d:2 * d]).astype(BF16)
    q = jnp.dot(h, wq_ref[...], preferred_element_type=F32) * (HEAD_DIM ** -0.5 * LOG2_E)
    q_ref[0] = q.astype(BF16)
    k_ref[0] = jnp.dot(h, wk_ref[...], preferred_element_type=F32).astype(BF16)
    vt = lax.dot_general(wvt_ref[...], h, (((1,), (1,)), ((), ())), preferred_element_type=F32)
    for j in range(vt_ref.shape[1]):
        vt_ref[0, j] = vt[:, j * LANES:(j + 1) * LANES].astype(BF16)


def _qkv(x, mod, g, wq, wk, wvt, tm):
    b, t, d = x.shape
    const = lambda i, j: (0, 0)
    return pl.pallas_call(
        _qkv_kernel,
        grid=(b, t // tm),
        in_specs=[
            pl.BlockSpec((1, tm, d), lambda i, j: (i, j, 0)),
            pl.BlockSpec((1, 1, 3 * d), lambda i, j: (i, 0, 0)),
            pl.BlockSpec((1, d), const),
            pl.BlockSpec((d, d), const),
            pl.BlockSpec((d, d), const),
            pl.BlockSpec((d, d), const),
        ],
        out_specs=[
            pl.BlockSpec((1, tm, d), lambda i, j: (i, j, 0)),
            pl.BlockSpec((1, tm, d), lambda i, j: (i, j, 0)),
            pl.BlockSpec((1, tm // LANES, d, LANES), lambda i, j: (i, j, 0, 0)),
        ],
        out_shape=[
            jax.ShapeDtypeStruct((b, t, d), BF16),
            jax.ShapeDtypeStruct((b, t, d), BF16),
            jax.ShapeDtypeStruct((b, t // LANES, d, LANES), BF16),
        ],
        compiler_params=_params("parallel", "parallel"),
        name="qkv",
    )(x, mod, g, wq, wk, wvt)


def _na_bias_table(rpb):
    h = rpb.shape[0]
    c = jnp.arange(GRID_W, dtype=jnp.int32)
    cs = jnp.clip(c - NA_KW // 2, 0, GRID_W - NA_KW)
    kc = c[:, None]
    valid = (kc >= cs[None, :]) & (kc < cs[None, :] + NA_KW)
    dc = jnp.clip(kc - c[None, :] + (NA_KW - 1), 0, 2 * NA_KW - 2)
    pick = (dc[None, :, :] == jnp.arange(2 * NA_KW - 1, dtype=jnp.int32)[:, None, None]).astype(F32)
    bt = jnp.einsum("hrd,dkc->hrkc", rpb.astype(F32), pick, precision=lax.Precision.HIGHEST)
    bt = jnp.where(valid, bt, MASK_BIAS)
    bt = jnp.concatenate([bt, jnp.full((h, 1, GRID_W, GRID_W), MASK_BIAS, F32)], axis=1)
    bt = bt.reshape(h // 2, 2, BIAS_TILES, GRID_W, GRID_W).transpose(0, 2, 3, 1, 4)
    return bt.reshape(h // 2, BIAS_TILES, GRID_W, 2 * GRID_W) * LOG2_E


def _na_kernel(q_ref, k_ref, vt_ref, bias_ref, o_ref, *, rows, pairs_per_step, row_pairs_per_step):
    step = pl.program_id(2)
    lane = lax.broadcasted_iota(jnp.int32, (ROW_PAIR, LANES), 1)
    first_head = lane < HEAD_DIM
    first_head_row = lax.broadcasted_iota(jnp.int32, (GRID_W, LANES), 1) < HEAD_DIM

    def row_pair(j, carry):
        rp = step * row_pairs_per_step + j
        r0 = 2 * rp
        wblk = jnp.clip(rp - NA_KR // 4, 0, (rows - WIN_ROWS) // 2)
        ws = 2 * wblk
        tok = j * ROW_PAIR
        ktok = pl.multiple_of(ws * GRID_W, ROW_PAIR)
        for p in range(pairs_per_step):
            cols = slice(p * LANES, (p + 1) * LANES)
            q2 = q_ref[0, pl.ds(tok, ROW_PAIR), cols]
            qa = jnp.where(first_head, q2, jnp.zeros_like(q2))
            qb = jnp.where(first_head, jnp.zeros_like(q2), q2)
            qbd = jnp.concatenate([qa[:GRID_W], qb[:GRID_W], qa[GRID_W:], qb[GRID_W:]], axis=0)
            kwin = k_ref[0, pl.ds(ktok, WIN_ROWS * GRID_W), cols]
            st = lax.dot_general(kwin, qbd, (((1,), (1,)), ((), ())), preferred_element_type=F32)
            blocks = []
            for i in range(WIN_ROWS):
                kr = ws + i
                tiles = []
                for s in range(2):
                    r = r0 + s
                    rs = jnp.clip(r - NA_KR // 2, 0, rows - NA_KR)
                    inside = jnp.logical_and(kr >= rs, kr < rs + NA_KR)
                    idx = jnp.where(inside, kr - r + (NA_KR - 1), BIAS_TILES - 1)
                    tiles.append(bias_ref[p, idx])
                blocks.append(st[i * GRID_W:(i + 1) * GRID_W] + jnp.concatenate(tiles, axis=1))
            sc = jnp.concatenate(blocks, axis=0)
            m = jnp.max(sc, axis=0, keepdims=True)
            e = jnp.exp2(sc - m)
            l = jnp.sum(e, axis=0, keepdims=True)
            pt = e.astype(BF16)
            ot = jnp.zeros((LANES, 2 * LANES), F32)
            for c in range(WIN_ROWS * GRID_W // LANES):
                vt = vt_ref[0, wblk + c, cols, :]
                ot = ot + jnp.dot(vt, pt[c * LANES:(c + 1) * LANES], preferred_element_type=F32)
            ot = ot * (1.0 / l)
            outs = []
            for s in range(2):
                tt = ot[:, s * LANES:(s + 1) * LANES].T
                outs.append(jnp.where(first_head_row, tt[:GRID_W], tt[GRID_W:]))
            o_ref[0, pl.ds(tok, ROW_PAIR), cols] = jnp.concatenate(outs, axis=0).astype(BF16)
        return carry

    for j in range(row_pairs_per_step):
        row_pair(j, 0)


def _na(q, k, vt, bias, *, head_groups, row_pairs_per_step):
    b, t, d = q.shape
    rows = t // GRID_W
    assert rows % 2 == 0 and rows >= WIN_ROWS and (rows // 2) % row_pairs_per_step == 0
    gc = d // head_groups
    pairs = gc // LANES
    tq = row_pairs_per_step * ROW_PAIR
    kern = functools.partial(_na_kernel, rows=rows, pairs_per_step=pairs, row_pairs_per_step=row_pairs_per_step)
    return pl.pallas_call(
        kern,
        grid=(b, head_groups, t // tq),
        in_specs=[
            pl.BlockSpec((1, tq, gc), lambda i, g, j: (i, j, g)),
            pl.BlockSpec((1, t, gc), lambda i, g, j: (i, 0, g)),
            pl.BlockSpec((1, t // LANES, gc, LANES), lambda i, g, j: (i, 0, g, 0)),
            pl.BlockSpec((pairs, BIAS_TILES, GRID_W, LANES), lambda i, g, j: (g, 0, 0, 0)),
        ],
        out_specs=pl.BlockSpec((1, tq, gc), lambda i, g, j: (i, j, g)),
        out_shape=jax.ShapeDtypeStruct((b, t, d), BF16),
        compiler_params=_params("parallel", "parallel", "parallel"),
        name="na_attn",
    )(q, k, vt, bias)


def _proj_ffn_kernel(x_ref, o_ref, mm_ref, mf_ref, g_ref, wo_ref, wgu_ref, wd_ref, out_ref, x1_s, h_s, acc_s):
    d = x_ref.shape[-1]
    f = pl.program_id(2)

    @pl.when(f == 0)
    def _():
        m = jnp.dot(o_ref[0], wo_ref[...], preferred_element_type=F32)
        x1 = x_ref[0] + mm_ref[0, :, 2 * d:3 * d] * m
        x1_s[...] = x1
        h_s[...] = _mod_rmsnorm(x1, g_ref[...], mf_ref[0, :, 0:d], mf_ref[0, :, d:2 * d]).astype(BF16)
        acc_s[...] = jnp.zeros_like(acc_s)

    acc_s[...] += _swiglu_step(h_s[...], wgu_ref[...], wd_ref[...])

    @pl.when(f == pl.num_programs(2) - 1)
    def _():
        out_ref[0] = x1_s[...] + mf_ref[0, :, 2 * d:3 * d] * acc_s[...]


def _proj_ffn(x, o, mod_mix, mod_ffn, g, wo, wgu, wd, tm, tf):
    b, t, d = x.shape
    ff = wd.shape[0]
    tok = lambda i, j, f: (i, j, 0)
    seq = lambda i, j, f: (i, 0, 0)
    return pl.pallas_call(
        _proj_ffn_kernel,
        grid=(b, t // tm, ff // tf),
        in_specs=[
            pl.BlockSpec((1, tm, d), tok),
            pl.BlockSpec((1, tm, d), tok),
            pl.BlockSpec((1, 1, 3 * d), seq),
            pl.BlockSpec((1, 1, 3 * d), seq),
            pl.BlockSpec((1, d), lambda i, j, f: (0, 0)),
            pl.BlockSpec((d, d), lambda i, j, f: (0, 0)),
            pl.BlockSpec((d, 2 * tf), lambda i, j, f: (0, f)),
            pl.BlockSpec((tf, d), lambda i, j, f: (f, 0)),
        ],
        out_specs=pl.BlockSpec((1, tm, d), tok),
        out_shape=jax.ShapeDtypeStruct((b, t, d), F32),
        scratch_shapes=[pltpu.VMEM((tm, d), F32), pltpu.VMEM((tm, d), BF16), pltpu.VMEM((tm, d), F32)],
        compiler_params=_params("parallel", "parallel", "arbitrary"),
        name="proj_ffn",
    )(x, o, mod_mix, mod_ffn, g, wo, wgu, wd)


def _pool_route_kernel(x_ref, xp_ref, xn_ref, mod_ref, g_ref, win_ref, wgrp_ref, ps_ref, wout_ref,
                       modf_ref, gf_ref, wr_ref, br_ref, out_ref, h_ref, w_ref, lp_ref, seg_ref, *, seq_len):
    d = x_ref.shape[-1]
    tm = x_ref.shape[1]
    halo = SUBLANES
    n = tm + 2 * halo
    j = pl.program_id(1)
    x = x_ref[0]
    xe = jnp.concatenate([xp_ref[0], x, xn_ref[0]], axis=0)
    h = _mod_rmsnorm(xe, g_ref[...], mod_ref[0, :, 0:d], mod_ref[0, :, d:2 * d]).astype(BF16)
    u = jnp.dot(h, win_ref[...], preferred_element_type=F32)
    pos = j * tm - halo + lax.broadcasted_iota(jnp.int32, (n, 1), 0)
    u = jnp.where(jnp.logical_and(pos >= 0, pos < seq_len), u, 0.0)
    gc = d // len(POOL_HALF)
    ys = []
    for gi, half in enumerate(POOL_HALF):
        ug = u[:, gi * gc:(gi + 1) * gc]
        fwd = ug
        span = 1
        while span < half:
            fwd = fwd + pltpu.roll(fwd, n - span, axis=0)
            span *= 2
        win = fwd + pltpu.roll(fwd, half, axis=0)
        cnt = jnp.clip(pos + half, 0, seq_len) - jnp.clip(pos - half, 0, seq_len)
        cnt = jnp.maximum(cnt, 1).astype(F32)
        pooled = (win / cnt - ug)[halo:halo + tm].astype(BF16)
        y = jnp.dot(pooled, wgrp_ref[gi], preferred_element_type=F32) * ps_ref[:, gi * gc:(gi + 1) * gc]
        ys.append(y.astype(BF16))
    m = jnp.dot(jnp.concatenate(ys, axis=1), wout_ref[...], preferred_element_type=F32)
    xo = x + mod_ref[0, :, 2 * d:3 * d] * m
    out_ref[0] = xo
    _route(xo, modf_ref, gf_ref, wr_ref, br_ref, h_ref, w_ref, lp_ref, seg_ref)


def _pool_route(x, mod, g, w_in, w_grp, pscale, w_out, mod_ffn, g_ffn, wr, br, tm):
    b, t, d = x.shape
    nt = t // tm
    hb = tm // SUBLANES
    last = t // SUBLANES - 1
    const2 = lambda i, j: (0, 0)
    tok = lambda i, j: (i, j, 0)
    return pl.pallas_call(
        functools.partial(_pool_route_kernel, seq_len=t),
        grid=(b, nt),
        in_specs=[
            pl.BlockSpec((1, tm, d), lambda i, j: (i, j, 0)),
            pl.BlockSpec((1, SUBLANES, d), lambda i, j: (i, jnp.maximum(j * hb - 1, 0), 0)),
            pl.BlockSpec((1, SUBLANES, d), lambda i, j: (i, jnp.minimum((j + 1) * hb, last), 0)),
            pl.BlockSpec((1, 1, 3 * d), lambda i, j: (i, 0, 0)),
            pl.BlockSpec((1, d), const2),
            pl.BlockSpec((d, d), const2),
            pl.BlockSpec(w_grp.shape, lambda i, j: (0, 0, 0)),
            pl.BlockSpec((1, d), const2),
            pl.BlockSpec((d, d), const2),
            pl.BlockSpec((1, 1, 3 * d), lambda i, j: (i, 0, 0)),
            pl.BlockSpec((1, d), const2),
            pl.BlockSpec((d, LANES), const2),
            pl.BlockSpec((1, LANES), const2),
        ],
        out_specs=[
            pl.BlockSpec((1, tm, d), tok),
            pl.BlockSpec((1, tm, d), tok),
            pl.BlockSpec((1, tm, LANES), tok),
            pl.BlockSpec((1, tm, LANES), tok),
            pl.BlockSpec((1, SUBLANES, LANES), lambda i, j: (i * nt + j, 0, 0)),
        ],
        out_shape=[
            jax.ShapeDtypeStruct((b, t, d), F32),
            jax.ShapeDtypeStruct((b, t, d), BF16),
            jax.ShapeDtypeStruct((b, t, LANES), F32),
            jax.ShapeDtypeStruct((b, t, LANES), F32),
            jax.ShapeDtypeStruct((b * nt, SUBLANES, LANES), jnp.int32),
        ],
        compiler_params=_params("parallel", "parallel"),
        name="pool_route",
    )(x, x, x, mod, g, w_in, w_grp, pscale, w_out, mod_ffn, g_ffn, wr, br)


def _group_rows(tm):
    return TOP_K * tm + N_EXPERTS * BF16_ROWS


def _route(x, mod_ref, g_ref, wr_ref, br_ref, h_ref, w_ref, lp_ref, seg_ref):
    tm, d = x.shape
    h = _mod_rmsnorm(x, g_ref[...], mod_ref[0, :, 0:d], mod_ref[0, :, d:2 * d]).astype(BF16)
    h_ref[0] = h
    logits = jnp.dot(h, wr_ref[...], preferred_element_type=F32) + br_ref[...]
    lane = lax.broadcasted_iota(jnp.int32, logits.shape, 1).astype(F32)
    v1 = jnp.max(logits, axis=-1, keepdims=True)
    i1 = jnp.min(jnp.where(logits == v1, lane, float(LANES)), axis=-1, keepdims=True)
    rest = jnp.where(lane == i1, -jnp.inf, logits)
    v2 = jnp.max(rest, axis=-1, keepdims=True)
    i2 = jnp.min(jnp.where(rest == v2, lane, float(LANES)), axis=-1, keepdims=True)
    e2 = jnp.exp(v2 - v1)
    w1 = 1.0 / (1.0 + e2)
    w2 = e2 / (1.0 + e2)
    sel = jnp.where(lane == i1, 1.0, jnp.where(lane == i2, 1.0, 0.0))
    row = lax.broadcasted_iota(jnp.int32, (tm, tm), 0)
    col = lax.broadcasted_iota(jnp.int32, (tm, tm), 1)
    before = jnp.where(col < row, 1.0, 0.0).astype(BF16)
    rank = jnp.dot(before, sel.astype(BF16), preferred_element_type=F32)
    count = jnp.sum(sel, axis=0, keepdims=True)
    tiles = jnp.floor((count + (BF16_ROWS - 1)) * (1.0 / BF16_ROWS))
    er = lax.broadcasted_iota(jnp.int32, (LANES, LANES), 0)
    ec = lax.broadcasted_iota(jnp.int32, (LANES, LANES), 1)
    lower = jnp.where(er < ec, 1.0, 0.0).astype(BF16)
    tiles8 = jnp.broadcast_to(tiles, (SUBLANES, LANES)).astype(BF16)
    start = jnp.dot(tiles8, lower, preferred_element_type=F32)[0:1] * BF16_ROWS
    slot = start + rank
    lp1 = jnp.sum(jnp.where(lane == i1, slot, 0.0), axis=-1, keepdims=True)
    lp2 = jnp.sum(jnp.where(lane == i2, slot, 0.0), axis=-1, keepdims=True)
    w_ref[0] = jnp.where(lane == 0, w1, jnp.where(lane == 1, w2, 0.0))
    lp_ref[0] = jnp.where(lane == 0, lp1, jnp.where(lane == 1, lp2, 0.0))
    sub = lax.broadcasted_iota(jnp.int32, (SUBLANES, LANES), 0)
    seg = jnp.where(sub == 0, tiles * BF16_ROWS, jnp.where(sub == 1, start, 0.0))
    seg_ref[0] = seg.astype(jnp.int32)


def _segment_copies(tile, dst_ref, n_ref, ls_ref, make_copy, act):
    for e in range(N_EXPERTS):
        n = n_ref[tile * N_EXPERTS + e]
        ls = ls_ref[tile * N_EXPERTS + e]
        dst = dst_ref[tile * N_EXPERTS + e]
        size = BF16_ROWS
        while size <= TOKEN_TILE:
            done = n & (-2 * size)

            @pl.when((n & size) != 0)
            def _(size=size, done=done, ls=ls, dst=dst):
                act(make_copy(pl.multiple_of(ls + done, BF16_ROWS), pl.multiple_of(dst + done, BF16_ROWS), size))

            size *= 2


def _dispatch_kernel(dst_ref, n_ref, ls_ref, h_ref, lp_ref, xs_ref, grp_s, zero_s, sem, seg_sem):
    tile = pl.program_id(0)
    n_tiles = pl.num_programs(0)
    tm = h_ref.shape[0]

    @pl.when(tile == 0)
    def _():
        zero_s[...] = jnp.zeros_like(zero_s)

        def make_fill(local, glob, size):
            return pltpu.make_async_copy(zero_s.at[pl.ds(local, size), :], xs_ref.at[pl.ds(glob, size), :], sem)

        _segment_copies(n_tiles, dst_ref, n_ref, ls_ref, make_fill, lambda c: c.start())
        _segment_copies(n_tiles, dst_ref, n_ref, ls_ref, make_fill, lambda c: c.wait())

        first = dst_ref[(n_tiles + 1) * N_EXPERTS]
        count = n_ref[(n_tiles + 1) * N_EXPERTS]

        def fill_tile(k):
            return make_fill(0, pl.multiple_of(first + k * tm, tm), tm)

        lax.fori_loop(0, count, lambda k, c: (fill_tile(k).start(), c)[1], 0)
        lax.fori_loop(0, count, lambda k, c: (fill_tile(k).wait(), c)[1], 0)

    gb = grp_s.shape[1]
    buf = tile % 2
    lpt = lp_ref[...].T
    slot = lax.broadcasted_iota(jnp.int32, (gb, tm), 0).astype(F32)
    onehot = jnp.where(slot == lpt[0:1, :], 1.0, jnp.where(slot == lpt[1:2, :], 1.0, 0.0)).astype(BF16)
    grp_s[buf] = jnp.dot(onehot, h_ref[...], preferred_element_type=F32).astype(BF16)

    def copies(t, b, act):
        def make_copy(local, glob, size):
            return pltpu.make_async_copy(grp_s.at[b, pl.ds(local, size), :], xs_ref.at[pl.ds(glob, size), :], seg_sem.at[b])

        _segment_copies(t, dst_ref, n_ref, ls_ref, make_copy, act)

    copies(tile, buf, lambda c: c.start())

    @pl.when(tile > 0)
    def _():
        copies(tile - 1, 1 - buf, lambda c: c.wait())

    @pl.when(tile == n_tiles - 1)
    def _():
        copies(tile, buf, lambda c: c.wait())


def _dispatch(h, lp, dst, seg_n, seg_ls, slots, tm):
    n, d = h.shape
    assert tm == TOKEN_TILE
    grid_spec = pltpu.PrefetchScalarGridSpec(
        num_scalar_prefetch=3,
        grid=(n // tm,),
        in_specs=[
            pl.BlockSpec((tm, d), lambda i, *_: (i, 0)),
            pl.BlockSpec((tm, LANES), lambda i, *_: (i, 0)),
        ],
        out_specs=pl.BlockSpec(memory_space=pl.ANY),
        scratch_shapes=[pltpu.VMEM((2, _group_rows(tm), d), BF16), pltpu.VMEM((tm, d), BF16),
                        pltpu.SemaphoreType.DMA(()), pltpu.SemaphoreType.DMA((2,))],
    )
    return pl.pallas_call(
        _dispatch_kernel,
        grid_spec=grid_spec,
        out_shape=jax.ShapeDtypeStruct((slots, d), BF16),
        compiler_params=_params("arbitrary"),
        name="moe_dispatch",
    )(dst, seg_n, seg_ls, h, lp)


def _swiglu_step(h, wgu, wd):
    tf = wd.shape[0]
    au = jnp.dot(h, wgu, preferred_element_type=F32)
    return jnp.dot((_silu(au[:, :tf]) * au[:, tf:]).astype(BF16), wd, preferred_element_type=F32)


def _experts_kernel(te_ref, tr_ref, ts_ref, xs_ref, wgu_ref, wd_ref, ys_ref, acc_s):
    del te_ref, ts_ref
    i = pl.program_id(0)
    f = pl.program_id(1)
    rows = tr_ref[i]
    part = xs_ref.shape[0] // EXPERT_TILE_PARTS

    @pl.when(f == 0)
    def _():
        acc_s[...] = jnp.zeros_like(acc_s)

    for k in range(1, EXPERT_TILE_PARTS + 1):
        @pl.when(jnp.logical_and(rows > (k - 1) * part, rows <= k * part))
        def _(used=k * part):
            acc_s[0:used, :] += _swiglu_step(xs_ref[0:used, :], wgu_ref[0], wd_ref[0])

    @pl.when(f == pl.num_programs(1) - 1)
    def _():
        ys_ref[...] = acc_s[...].astype(BF16)


def _experts(xs, tile_expert, tile_rows, tile_src, wgu, wd, tm, tf):
    slots, d = xs.shape
    ff = wd.shape[1]
    nf = ff // tf
    fidx = lambda f, tr, i: jnp.where(tr[i] > 0, f, nf - 1)
    grid_spec = pltpu.PrefetchScalarGridSpec(
        num_scalar_prefetch=3,
        grid=(slots // tm, nf),
        in_specs=[
            pl.BlockSpec((tm, d), lambda i, f, te, tv, ts: (ts[i], 0)),
            pl.BlockSpec((1, d, 2 * tf), lambda i, f, te, tv, ts: (te[i], 0, fidx(f, tv, i))),
            pl.BlockSpec((1, tf, d), lambda i, f, te, tv, ts: (te[i], fidx(f, tv, i), 0)),
        ],
        out_specs=pl.BlockSpec((tm, d), lambda i, f, te, tv, ts: (i, 0)),
        scratch_shapes=[pltpu.VMEM((tm, d), F32)],
    )
    return pl.pallas_call(
        _experts_kernel,
        grid_spec=grid_spec,
        out_shape=jax.ShapeDtypeStruct((slots, d), BF16),
        compiler_params=_params("parallel", "arbitrary"),
        name="moe_experts",
    )(tile_expert, tile_rows, tile_src, xs, wgu, wd)


def _combine_kernel(dst_ref, n_ref, ls_ref, x_ref, w_ref, lp_ref, mod_ref, gf_ref, ys_ref, out_ref, y_s, sem):
    d = x_ref.shape[-1]
    tm = x_ref.shape[1]
    gb = y_s.shape[1]
    tile = pl.program_id(0) * pl.num_programs(1) + pl.program_id(1)
    n_tiles = pl.num_programs(0) * pl.num_programs(1)
    buf = tile % 2

    def fetch(t, b, act):
        def make_copy(local, glob, size):
            return pltpu.make_async_copy(ys_ref.at[pl.ds(glob, size), :], y_s.at[b, pl.ds(local, size), :], sem.at[b])

        _segment_copies(t, dst_ref, n_ref, ls_ref, make_copy, act)

    def start_fetch(t, b):
        y_s[b, TOP_K * tm:, :] = jnp.zeros((gb - TOP_K * tm, d), BF16)
        fetch(t, b, lambda c: c.start())

    @pl.when(tile == 0)
    def _():
        start_fetch(tile, buf)

    @pl.when(tile + 1 < n_tiles)
    def _():
        start_fetch(tile + 1, 1 - buf)

    fetch(tile, buf, lambda c: c.wait())

    y = y_s[buf]
    lp = lp_ref[0]
    w = w_ref[0]
    slot = lax.broadcasted_iota(jnp.int32, (tm, gb), 1).astype(F32)
    y1 = jnp.dot(jnp.where(slot == lp[:, 0:1], 1.0, 0.0).astype(BF16), y, preferred_element_type=F32)
    y2 = jnp.dot(jnp.where(slot == lp[:, 1:2], 1.0, 0.0).astype(BF16), y, preferred_element_type=F32)
    f = w[:, 0:1] * y1 + w[:, 1:2] * y2
    x = x_ref[0] + mod_ref[0, :, 2 * d:3 * d] * f
    var = jnp.mean(x * x, axis=-1, keepdims=True)
    out_ref[0] = (x * lax.rsqrt(var + RMS_EPS)) * gf_ref[...]


def _combine(x, w, lp, mod, gf, ys, dst, seg_n, seg_ls, tm):
    b, t, d = x.shape
    assert tm == TOKEN_TILE
    tok = lambda i, j, *_: (i, j, 0)
    grid_spec = pltpu.PrefetchScalarGridSpec(
        num_scalar_prefetch=3,
        grid=(b, t // tm),
        in_specs=[
            pl.BlockSpec((1, tm, d), tok),
            pl.BlockSpec((1, tm, LANES), tok),
            pl.BlockSpec((1, tm, LANES), tok),
            pl.BlockSpec((1, 1, 3 * d), lambda i, j, *_: (i, 0, 0)),
            pl.BlockSpec((1, d), lambda i, j, *_: (0, 0)),
            pl.BlockSpec(memory_space=pl.ANY),
        ],
        out_specs=pl.BlockSpec((1, tm, d), tok),
        scratch_shapes=[pltpu.VMEM((2, _group_rows(tm), d), BF16), pltpu.SemaphoreType.DMA((2,))],
    )
    return pl.pallas_call(
        _combine_kernel,
        grid_spec=grid_spec,
        out_shape=jax.ShapeDtypeStruct((b, t, d), F32),
        compiler_params=_params("arbitrary", "arbitrary"),
        name="moe_combine",
    )(dst, seg_n, seg_ls, x, w, lp, mod, gf, ys)


def _trunk(x, mods_mix, mods_ffn, p):
    b, t, d = x.shape
    tm = TOKEN_TILE
    assert t % tm == 0

    q, k, vt = _qkv(x, mods_mix[0], p["ln_mix_g"][0], p["wq"], p["wk"], p["wvt"], tm)
    rows = t // GRID_W
    rpp = 8 if (rows // 2) % 8 == 0 else 1
    o = _na(q, k, vt, p["na_bias"], head_groups=2, row_pairs_per_step=rpp)
    x = _proj_ffn(x, o, mods_mix[0], mods_ffn[0], p["ln_ffn_g"][0], p["wo"], p["ffn_wgu"], p["ffn_wd"], tm, p["tf_dense"])

    x, h, w, lp, seg = _pool_route(x, mods_mix[1], p["ln_mix_g"][1], p["pool_w_in"], p["pool_w_grp"], p["pool_scale"],
                                   p["pool_w_out"], mods_ffn[1], p["ln_ffn_g"][1], p["wr"], p["br"], tm)

    n = b * t
    nt = n // tm
    seg_n = seg[:, 0, :N_EXPERTS]
    seg_ls = seg[:, 1, :N_EXPERTS]
    before = jnp.cumsum(seg_n, axis=0) - seg_n
    total = jnp.sum(seg_n, axis=0)
    group = ((total + tm - 1) // tm) * tm
    ends = jnp.cumsum(group)
    dst = (ends - group)[None, :] + before
    n_tiles = -(-(TOP_K * n + nt * N_EXPERTS * (BF16_ROWS - 1)) // tm) + N_EXPERTS
    tile_start = jnp.arange(n_tiles, dtype=jnp.int32) * tm
    tile_live = (tile_start < ends[-1]).astype(jnp.int32)
    tile_expert = jnp.sum((ends[None, :] <= tile_start[:, None]).astype(jnp.int32), axis=1)
    last_expert = jnp.sum((ends <= ends[-1] - 1).astype(jnp.int32))
    tile_expert = jnp.minimum(jnp.where(tile_live > 0, tile_expert, last_expert), N_EXPERTS - 1)
    tile_src = jnp.minimum(jnp.arange(n_tiles, dtype=jnp.int32), jnp.sum(tile_live) - 1)
    filled_end = (ends - group + total)[tile_expert]
    tile_rows = jnp.where(tile_live > 0, jnp.clip(filled_end - tile_start, 0, tm), 0).astype(jnp.int32)
    lead = jnp.arange(N_EXPERTS) == 0
    dst = jnp.concatenate([dst, (ends - group + total)[None, :], jnp.where(lead, ends[-1], 0)[None, :]], axis=0)
    seg_n = jnp.concatenate([seg_n, (group - total)[None, :], jnp.where(lead, n_tiles - jnp.sum(tile_live), 0)[None, :]], axis=0)
    seg_ls = jnp.concatenate([seg_ls, jnp.zeros((2, N_EXPERTS), seg_ls.dtype)], axis=0)
    dst, seg_n, seg_ls = (a.reshape(-1).astype(jnp.int32) for a in (dst, seg_n, seg_ls))

    xs = _dispatch(h.reshape(n, d), lp.reshape(n, LANES), dst, seg_n, seg_ls, n_tiles * tm, tm)
    ys = _experts(xs, tile_expert, tile_rows, tile_src, p["moe_wgu"], p["moe_wd"], tm, p["tf_moe"])
    return _combine(x, w, lp, mods_ffn[1], p["ln_f_g"], ys, dst, seg_n, seg_ls, tm)


def kernel(x_prompt, x_sample, c_prompt, c_sample, ln_mix_g, ada_mix_w, ada_mix_b, ln_ffn_g, ada_ffn_w, ada_ffn_b, na_w_qkv, na_rpb, na_w_o, pool_w_in, pool_w_grp, pool_scale, pool_w_out, ffn_w_gate, ffn_w_up, ffn_w_down, moe_w_router, moe_b_router, moe_w_gate, moe_w_up, moe_w_down, ln_f_g):
    d = x_prompt.shape[-1]
    depth = ln_mix_g.shape[0]
    assert depth == 2 and d == NA_HEADS * HEAD_DIM
    bp, bs = c_prompt.shape[0], c_sample.shape[0]

    pad = (-(bp + bs)) % SUBLANES
    c_all = jnp.concatenate([c_prompt, c_sample, jnp.zeros((pad, d), F32)], axis=0)
    mods_mix = _adaln(c_all, ada_mix_w, ada_mix_b)
    mods_ffn = _adaln(c_all, ada_ffn_w, ada_ffn_b)

    wqkv = na_w_qkv[0]
    ff = ffn_w_gate.shape[-1]
    tf_dense = _largest_tile(ff, FF_TILE_DENSE)
    tf_moe = _largest_tile(ff, FF_TILE_MOE)
    ne = moe_w_router.shape[-1]
    assert ne == N_EXPERTS
    p = {
        "ln_mix_g": ln_mix_g.reshape(depth, 1, d),
        "ln_ffn_g": ln_ffn_g.reshape(depth, 1, d),
        "ln_f_g": ln_f_g.reshape(1, d),
        "wq": wqkv[:, 0:d].astype(BF16),
        "wk": wqkv[:, d:2 * d].astype(BF16),
        "wvt": wqkv[:, 2 * d:3 * d].T.astype(BF16),
        "na_bias": _na_bias_table(na_rpb[0]),
        "wo": na_w_o[0].astype(BF16),
        "tf_dense": tf_dense,
        "tf_moe": tf_moe,
        "ffn_wgu": _fuse_gate_up(ffn_w_gate[0], ffn_w_up[0], tf_dense),
        "ffn_wd": ffn_w_down[0].astype(BF16),
        "pool_w_in": pool_w_in[0].astype(BF16),
        "pool_w_grp": pool_w_grp[0].astype(BF16),
        "pool_scale": pool_scale[0].reshape(1, d),
        "pool_w_out": pool_w_out[0].astype(BF16),
        "wr": jnp.pad(moe_w_router[0], ((0, 0), (0, LANES - ne))).astype(BF16),
        "br": jnp.pad(moe_b_router[0].astype(F32), (0, LANES - ne), constant_values=MASK_BIAS).reshape(1, LANES),
        "moe_wgu": _fuse_gate_up(moe_w_gate[0], moe_w_up[0], tf_moe),
        "moe_wd": moe_w_down[0].astype(BF16),
    }

    def group_mods(m, lo, n):
        return m[:, lo:lo + n, None, :]

    y_prompt = _trunk(x_prompt, group_mods(mods_mix, 0, bp), group_mods(mods_ffn, 0, bp), p)
    y_sample = _trunk(x_sample, group_mods(mods_mix, bp, bs), group_mods(mods_ffn, bp, bs), p)
    return (y_prompt, y_sample)
```

```python
import functools

import jax
import jax.numpy as jnp
from jax import lax
from jax.experimental import pallas as pl
from jax.experimental.pallas import tpu as pltpu

F32 = jnp.float32
BF16 = jnp.bfloat16

RMS_EPS = 1e-6
GRID_W = 64
NA_HEADS = 16
HEAD_DIM = 64
NA_KR = 8
NA_KW = 16
POOL_HALF = (1, 2, 4, 8)
N_EXPERTS = 8
TOP_K = 2
LANES = 128
SUBLANES = 8
BF16_ROWS = 16
MASK_BIAS = -1e30
LOG2_E = 1.4426950408889634
ROW_PAIR = 2 * GRID_W
WIN_ROWS = NA_KR + 2
BIAS_TILES = 2 * NA_KR
VMEM_LIMIT = 56 * 1024 * 1024
TOKEN_TILE = 512
FF_TILE_DENSE = 1792
FF_TILE_MOE = 896
EXPERT_TILE = 1024
EXPERT_TILE_PARTS = 4


def _silu(a):
    return a * jax.nn.sigmoid(a)


def _mod_rmsnorm(x, g, shift, scale):
    var = jnp.mean(x * x, axis=-1, keepdims=True)
    y = (x * lax.rsqrt(var + RMS_EPS)) * g
    return y * (1.0 + scale) + shift


def _params(*sem, vmem=VMEM_LIMIT):
    return pltpu.CompilerParams(dimension_semantics=sem, vmem_limit_bytes=vmem)


def _largest_tile(total, target):
    best = LANES
    for cand in range(LANES, min(total, target) + 1, LANES):
        if total % cand == 0:
            best = cand
    assert total % best == 0
    return best


def _adaln_kernel(c_ref, w_ref, b_ref, o_ref):
    s = _silu(c_ref[...]).astype(BF16)
    o_ref[0] = jnp.dot(s, w_ref[0].astype(BF16), preferred_element_type=F32) + b_ref[0]


def _adaln(c_all, w, b):
    depth, d, d3 = w.shape
    r = c_all.shape[0]
    tn = _largest_tile(d3, 1024)
    return pl.pallas_call(
        _adaln_kernel,
        grid=(depth, d3 // tn),
        in_specs=[
            pl.BlockSpec((r, d), lambda i, j: (0, 0)),
            pl.BlockSpec((1, d, tn), lambda i, j: (i, 0, j)),
            pl.BlockSpec((1, 1, tn), lambda i, j: (i, 0, j)),
        ],
        out_specs=pl.BlockSpec((1, r, tn), lambda i, j: (i, 0, j)),
        out_shape=jax.ShapeDtypeStruct((depth, r, d3), F32),
        compiler_params=_params("parallel", "parallel"),
        name="adaln",
    )(c_all, w, b.reshape(depth, 1, d3))


def _qkv_kernel(x_ref, mod_ref, g_ref, wq_ref, wk_ref, wvt_ref, q_ref, k_ref, vt_ref):
    d = x_ref.shape[-1]
    h = _mod_rmsnorm(x_ref[0], g_ref[...], mod_ref[0, :, 0:d], mod_ref[0, :, d:2 * d]).astype(BF16)
    q = jnp.dot(h, wq_ref[...], preferred_element_type=F32) * (HEAD_DIM ** -0.5 * LOG2_E)
    q_ref[0] = q.astype(BF16)
    k_ref[0] = jnp.dot(h, wk_ref[...], preferred_element_type=F32).astype(BF16)
    vt = lax.dot_general(wvt_ref[...], h, (((1,), (1,)), ((), ())), preferred_element_type=F32)
    for j in range(vt_ref.shape[1]):
        vt_ref[0, j] = vt[:, j * LANES:(j + 1) * LANES].astype(BF16)


def _qkv(x, mod, g, wq, wk, wvt, tm):
    b, t, d = x.shape
    const = lambda i, j: (0, 0)
    return pl.pallas_call(
        _qkv_kernel,
        grid=(b, t // tm),
        in_specs=[
            pl.BlockSpec((1, tm, d), lambda i, j: (i, j, 0)),
            pl.BlockSpec((1, 1, 3 * d), lambda i, j: (i, 0, 0)),
            pl.BlockSpec((1, d), const),
            pl.BlockSpec((d, d), const),
            pl.BlockSpec((d, d), const),
            pl.BlockSpec((d, d), const),
        ],
        out_specs=[
            pl.BlockSpec((1, tm, d), lambda i, j: (i, j, 0)),
            pl.BlockSpec((1, tm, d), lambda i, j: (i, j, 0)),
            pl.BlockSpec((1, tm // LANES, d, LANES), lambda i, j: (i, j, 0, 0)),
        ],
        out_shape=[
            jax.ShapeDtypeStruct((b, t, d), BF16),
            jax.ShapeDtypeStruct((b, t, d), BF16),
            jax.ShapeDtypeStruct((b, t // LANES, d, LANES), BF16),
        ],
        compiler_params=_params("parallel", "parallel"),
        name="qkv",
    )(x, mod, g, wq, wk, wvt)


def _na_bias_table(rpb):
    h = rpb.shape[0]
    c = jnp.arange(GRID_W, dtype=jnp.int32)
    cs = jnp.clip(c - NA_KW // 2, 0, GRID_W - NA_KW)
    kc = c[:, None]
    valid = (kc >= cs[None, :]) & (kc < cs[None, :] + NA_KW)
    dc = jnp.clip(kc - c[None, :] + (NA_KW - 1), 0, 2 * NA_KW - 2)
    bt = jnp.where(valid, rpb.astype(F32)[:, :, dc], MASK_BIAS)
    bt = jnp.concatenate([bt, jnp.full((h, 1, GRID_W, GRID_W), MASK_BIAS, F32)], axis=1)
    bt = bt.reshape(h // 2, 2, BIAS_TILES, GRID_W, GRID_W).transpose(0, 2, 3, 1, 4)
    return bt.reshape(h // 2, BIAS_TILES, GRID_W, 2 * GRID_W) * LOG2_E


def _na_kernel(q_ref, k_ref, vt_ref, bias_ref, o_ref, *, rows, pairs_per_step, row_pairs_per_step):
    step = pl.program_id(2)
    lane = lax.broadcasted_iota(jnp.int32, (ROW_PAIR, LANES), 1)
    first_head = lane < HEAD_DIM
    first_head_row = lax.broadcasted_iota(jnp.int32, (GRID_W, LANES), 1) < HEAD_DIM

    def row_pair(j, carry):
        rp = step * row_pairs_per_step + j
        r0 = 2 * rp
        wblk = jnp.clip(rp - NA_KR // 4, 0, (rows - WIN_ROWS) // 2)
        ws = 2 * wblk
        tok = j * ROW_PAIR
        ktok = pl.multiple_of(ws * GRID_W, ROW_PAIR)
        for p in range(pairs_per_step):
            cols = slice(p * LANES, (p + 1) * LANES)
            q2 = q_ref[0, pl.ds(tok, ROW_PAIR), cols]
            qa = jnp.where(first_head, q2, jnp.zeros_like(q2))
            qb = jnp.where(first_head, jnp.zeros_like(q2), q2)
            qbd = jnp.concatenate([qa[:GRID_W], qb[:GRID_W], qa[GRID_W:], qb[GRID_W:]], axis=0)
            kwin = k_ref[0, pl.ds(ktok, WIN_ROWS * GRID_W), cols]
            st = lax.dot_general(kwin, qbd, (((1,), (1,)), ((), ())), preferred_element_type=F32)
            blocks = []
            for i in range(WIN_ROWS):
                kr = ws + i
                tiles = []
                for s in range(2):
                    r = r0 + s
                    rs = jnp.clip(r - NA_KR // 2, 0, rows - NA_KR)
                    inside = jnp.logical_and(kr >= rs, kr < rs + NA_KR)
                    idx = jnp.where(inside, kr - r + (NA_KR - 1), BIAS_TILES - 1)
                    tiles.append(bias_ref[p, idx])
                blocks.append(st[i * GRID_W:(i + 1) * GRID_W] + jnp.concatenate(tiles, axis=1))
            sc = jnp.concatenate(blocks, axis=0)
            m = jnp.max(sc, axis=0, keepdims=True)
            e = jnp.exp2(sc - m)
            l = jnp.sum(e, axis=0, keepdims=True)
            pt = e.astype(BF16)
            ot = jnp.zeros((LANES, 2 * LANES), F32)
            for c in range(WIN_ROWS * GRID_W // LANES):
                vt = vt_ref[0, wblk + c, cols, :]
                ot = ot + jnp.dot(vt, pt[c * LANES:(c + 1) * LANES], preferred_element_type=F32)
            ot = ot * (1.0 / l)
            outs = []
            for s in range(2):
                tt = ot[:, s * LANES:(s + 1) * LANES].T
                outs.append(jnp.where(first_head_row, tt[:GRID_W], tt[GRID_W:]))
            o_ref[0, pl.ds(tok, ROW_PAIR), cols] = jnp.concatenate(outs, axis=0).astype(BF16)
        return carry

    for j in range(row_pairs_per_step):
        row_pair(j, 0)


def _na(q, k, vt, bias, *, head_groups, row_pairs_per_step):
    b, t, d = q.shape
    rows = t // GRID_W
    assert rows % 2 == 0 and rows >= WIN_ROWS and (rows // 2) % row_pairs_per_step == 0
    gc = d // head_groups
    pairs = gc // LANES
    tq = row_pairs_per_step * ROW_PAIR
    kern = functools.partial(_na_kernel, rows=rows, pairs_per_step=pairs, row_pairs_per_step=row_pairs_per_step)
    return pl.pallas_call(
        kern,
        grid=(b, head_groups, t // tq),
        in_specs=[
            pl.BlockSpec((1, tq, gc), lambda i, g, j: (i, j, g)),
            pl.BlockSpec((1, t, gc), lambda i, g, j: (i, 0, g)),
            pl.BlockSpec((1, t // LANES, gc, LANES), lambda i, g, j: (i, 0, g, 0)),
            pl.BlockSpec((pairs, BIAS_TILES, GRID_W, LANES), lambda i, g, j: (g, 0, 0, 0)),
        ],
        out_specs=pl.BlockSpec((1, tq, gc), lambda i, g, j: (i, j, g)),
        out_shape=jax.ShapeDtypeStruct((b, t, d), BF16),
        compiler_params=_params("parallel", "parallel", "parallel"),
        name="na_attn",
    )(q, k, vt, bias)


def _proj_ffn_kernel(x_ref, o_ref, mm_ref, mf_ref, g_ref, wo_ref, wg_ref, wu_ref, wd_ref, out_ref, x1_s, h_s, acc_s):
    d = x_ref.shape[-1]
    f = pl.program_id(2)

    @pl.when(f == 0)
    def _():
        m = jnp.dot(o_ref[0], wo_ref[...], preferred_element_type=F32)
        x1 = x_ref[0] + mm_ref[0, :, 2 * d:3 * d] * m
        x1_s[...] = x1
        h_s[...] = _mod_rmsnorm(x1, g_ref[...], mf_ref[0, :, 0:d], mf_ref[0, :, d:2 * d]).astype(BF16)
        acc_s[...] = jnp.zeros_like(acc_s)

    acc_s[...] += _swiglu_step(h_s[...], wg_ref[...], wu_ref[...], wd_ref[...])

    @pl.when(f == pl.num_programs(2) - 1)
    def _():
        out_ref[0] = x1_s[...] + mf_ref[0, :, 2 * d:3 * d] * acc_s[...]


def _proj_ffn(x, o, mod_mix, mod_ffn, g, wo, wg, wu, wd, tm, tf):
    b, t, d = x.shape
    ff = wd.shape[0]
    tok = lambda i, j, f: (i, j, 0)
    seq = lambda i, j, f: (i, 0, 0)
    return pl.pallas_call(
        _proj_ffn_kernel,
        grid=(b, t // tm, ff // tf),
        in_specs=[
            pl.BlockSpec((1, tm, d), tok),
            pl.BlockSpec((1, tm, d), tok),
            pl.BlockSpec((1, 1, 3 * d), seq),
            pl.BlockSpec((1, 1, 3 * d), seq),
            pl.BlockSpec((1, d), lambda i, j, f: (0, 0)),
            pl.BlockSpec((d, d), lambda i, j, f: (0, 0)),
            pl.BlockSpec((d, tf), lambda i, j, f: (0, f)),
            pl.BlockSpec((d, tf), lambda i, j, f: (0, f)),
            pl.BlockSpec((tf, d), lambda i, j, f: (f, 0)),
        ],
        out_specs=pl.BlockSpec((1, tm, d), tok),
        out_shape=jax.ShapeDtypeStruct((b, t, d), F32),
        scratch_shapes=[pltpu.VMEM((tm, d), F32), pltpu.VMEM((tm, d), BF16), pltpu.VMEM((tm, d), F32)],
        compiler_params=_params("parallel", "parallel", "arbitrary"),
        name="proj_ffn",
    )(x, o, mod_mix, mod_ffn, g, wo, wg, wu, wd)


def _pool_route_kernel(x_ref, xp_ref, xn_ref, mod_ref, g_ref, win_ref, wgrp_ref, ps_ref, wout_ref,
                       modf_ref, gf_ref, wr_ref, br_ref, out_ref, h_ref, w_ref, lp_ref, seg_ref, *, seq_len):
    d = x_ref.shape[-1]
    tm = x_ref.shape[1]
    halo = SUBLANES
    n = tm + 2 * halo
    j = pl.program_id(1)
    x = x_ref[0]
    xe = jnp.concatenate([xp_ref[0], x, xn_ref[0]], axis=0)
    h = _mod_rmsnorm(xe, g_ref[...], mod_ref[0, :, 0:d], mod_ref[0, :, d:2 * d]).astype(BF16)
    u = jnp.dot(h, win_ref[...], preferred_element_type=F32)
    pos = j * tm - halo + lax.broadcasted_iota(jnp.int32, (n, 1), 0)
    u = jnp.where(jnp.logical_and(pos >= 0, pos < seq_len), u, 0.0)
    gc = d // len(POOL_HALF)
    ys = []
    for gi, half in enumerate(POOL_HALF):
        ug = u[:, gi * gc:(gi + 1) * gc]
        fwd = ug
        span = 1
        while span < half:
            fwd = fwd + pltpu.roll(fwd, n - span, axis=0)
            span *= 2
        win = fwd + pltpu.roll(fwd, half, axis=0)
        cnt = jnp.clip(pos + half, 0, seq_len) - jnp.clip(pos - half, 0, seq_len)
        cnt = jnp.maximum(cnt, 1).astype(F32)
        pooled = (win / cnt - ug)[halo:halo + tm].astype(BF16)
        y = jnp.dot(pooled, wgrp_ref[gi], preferred_element_type=F32) * ps_ref[:, gi * gc:(gi + 1) * gc]
        ys.append(y.astype(BF16))
    m = jnp.dot(jnp.concatenate(ys, axis=1), wout_ref[...], preferred_element_type=F32)
    xo = x + mod_ref[0, :, 2 * d:3 * d] * m
    out_ref[0] = xo
    _route(xo, modf_ref, gf_ref, wr_ref, br_ref, h_ref, w_ref, lp_ref, seg_ref)


def _pool_route(x, mod, g, w_in, w_grp, pscale, w_out, mod_ffn, g_ffn, wr, br, tm):
    b, t, d = x.shape
    nt = t // tm
    hb = tm // SUBLANES
    last = t // SUBLANES - 1
    const2 = lambda i, j: (0, 0)
    tok = lambda i, j: (i, j, 0)
    return pl.pallas_call(
        functools.partial(_pool_route_kernel, seq_len=t),
        grid=(b, nt),
        in_specs=[
            pl.BlockSpec((1, tm, d), lambda i, j: (i, j, 0)),
            pl.BlockSpec((1, SUBLANES, d), lambda i, j: (i, jnp.maximum(j * hb - 1, 0), 0)),
            pl.BlockSpec((1, SUBLANES, d), lambda i, j: (i, jnp.minimum((j + 1) * hb, last), 0)),
            pl.BlockSpec((1, 1, 3 * d), lambda i, j: (i, 0, 0)),
            pl.BlockSpec((1, d), const2),
            pl.BlockSpec((d, d), const2),
            pl.BlockSpec(w_grp.shape, lambda i, j: (0, 0, 0)),
            pl.BlockSpec((1, d), const2),
            pl.BlockSpec((d, d), const2),
            pl.BlockSpec((1, 1, 3 * d), lambda i, j: (i, 0, 0)),
            pl.BlockSpec((1, d), const2),
            pl.BlockSpec((d, LANES), const2),
            pl.BlockSpec((1, LANES), const2),
        ],
        out_specs=[
            pl.BlockSpec((1, tm, d), tok),
            pl.BlockSpec((1, tm, d), tok),
            pl.BlockSpec((1, tm, LANES), tok),
            pl.BlockSpec((1, tm, LANES), tok),
            pl.BlockSpec((1, SUBLANES, LANES), lambda i, j: (i * nt + j, 0, 0)),
        ],
        out_shape=[
            jax.ShapeDtypeStruct((b, t, d), F32),
            jax.ShapeDtypeStruct((b, t, d), BF16),
            jax.ShapeDtypeStruct((b, t, LANES), F32),
            jax.ShapeDtypeStruct((b, t, LANES), F32),
            jax.ShapeDtypeStruct((b * nt, SUBLANES, LANES), jnp.int32),
        ],
        compiler_params=_params("parallel", "parallel"),
        name="pool_route",
    )(x, x, x, mod, g, w_in, w_grp, pscale, w_out, mod_ffn, g_ffn, wr, br)


def _group_rows(tm):
    return TOP_K * tm + N_EXPERTS * BF16_ROWS


def _route(x, mod_ref, g_ref, wr_ref, br_ref, h_ref, w_ref, lp_ref, seg_ref):
    tm, d = x.shape
    h = _mod_rmsnorm(x, g_ref[...], mod_ref[0, :, 0:d], mod_ref[0, :, d:2 * d]).astype(BF16)
    h_ref[0] = h
    logits = jnp.dot(h, wr_ref[...], preferred_element_type=F32) + br_ref[...]
    lane = lax.broadcasted_iota(jnp.int32, logits.shape, 1).astype(F32)
    v1 = jnp.max(logits, axis=-1, keepdims=True)
    i1 = jnp.min(jnp.where(logits == v1, lane, float(LANES)), axis=-1, keepdims=True)
    rest = jnp.where(lane == i1, -jnp.inf, logits)
    v2 = jnp.max(rest, axis=-1, keepdims=True)
    i2 = jnp.min(jnp.where(rest == v2, lane, float(LANES)), axis=-1, keepdims=True)
    e2 = jnp.exp(v2 - v1)
    w1 = 1.0 / (1.0 + e2)
    w2 = e2 / (1.0 + e2)
    sel = jnp.where(lane == i1, 1.0, jnp.where(lane == i2, 1.0, 0.0))
    row = lax.broadcasted_iota(jnp.int32, (tm, tm), 0)
    col = lax.broadcasted_iota(jnp.int32, (tm, tm), 1)
    before = jnp.where(col < row, 1.0, 0.0).astype(BF16)
    rank = jnp.dot(before, sel.astype(BF16), preferred_element_type=F32)
    count = jnp.sum(sel, axis=0, keepdims=True)
    tiles = jnp.floor((count + (BF16_ROWS - 1)) * (1.0 / BF16_ROWS))
    er = lax.broadcasted_iota(jnp.int32, (LANES, LANES), 0)
    ec = lax.broadcasted_iota(jnp.int32, (LANES, LANES), 1)
    lower = jnp.where(er < ec, 1.0, 0.0).astype(BF16)
    tiles8 = jnp.broadcast_to(tiles, (SUBLANES, LANES)).astype(BF16)
    start = jnp.dot(tiles8, lower, preferred_element_type=F32)[0:1] * BF16_ROWS
    slot = start + rank
    lp1 = jnp.sum(jnp.where(lane == i1, slot, 0.0), axis=-1, keepdims=True)
    lp2 = jnp.sum(jnp.where(lane == i2, slot, 0.0), axis=-1, keepdims=True)
    w_ref[0] = jnp.where(lane == 0, w1, jnp.where(lane == 1, w2, 0.0))
    lp_ref[0] = jnp.where(lane == 0, lp1, jnp.where(lane == 1, lp2, 0.0))
    sub = lax.broadcasted_iota(jnp.int32, (SUBLANES, LANES), 0)
    seg = jnp.where(sub == 0, tiles * BF16_ROWS, jnp.where(sub == 1, start, 0.0))
    seg_ref[0] = seg.astype(jnp.int32)


def _segment_copies(tile, dst_ref, n_ref, ls_ref, make_copy, act):
    for e in range(N_EXPERTS):
        n = n_ref[tile * N_EXPERTS + e]
        ls = ls_ref[tile * N_EXPERTS + e]
        dst = dst_ref[tile * N_EXPERTS + e]
        size = BF16_ROWS
        while size <= TOKEN_TILE:
            done = n & (-2 * size)

            @pl.when((n & size) != 0)
            def _(size=size, done=done, ls=ls, dst=dst):
                act(make_copy(pl.multiple_of(ls + done, BF16_ROWS), pl.multiple_of(dst + done, BF16_ROWS), size))

            size *= 2


def _dispatch_kernel(dst_ref, n_ref, ls_ref, h_ref, lp_ref, xs_ref, grp_s, zero_s, sem, seg_sem):
    tile = pl.program_id(0)
    n_tiles = pl.num_programs(0)
    tm = h_ref.shape[0]

    @pl.when(tile == 0)
    def _():
        zero_s[...] = jnp.zeros_like(zero_s)

        def make_fill(local, glob, size):
            del local
            return pltpu.make_async_copy(zero_s.at[pl.ds(0, size), :], xs_ref.at[pl.ds(glob, size), :], sem)

        _segment_copies(n_tiles, dst_ref, n_ref, ls_ref, make_fill, lambda c: c.start())
        _segment_copies(n_tiles, dst_ref, n_ref, ls_ref, make_fill, lambda c: c.wait())

        first = dst_ref[(n_tiles + 1) * N_EXPERTS]
        count = n_ref[(n_tiles + 1) * N_EXPERTS]

        def fill_tile(k):
            return make_fill(0, pl.multiple_of(first + k * tm, tm), tm)

        lax.fori_loop(0, count, lambda k, c: (fill_tile(k).start(), c)[1], 0)
        lax.fori_loop(0, count, lambda k, c: (fill_tile(k).wait(), c)[1], 0)

    gb = grp_s.shape[1]
    buf = tile % 2
    lpt = lp_ref[...].T
    slot = lax.broadcasted_iota(jnp.int32, (gb, tm), 0).astype(F32)
    onehot = jnp.where(slot == lpt[0:1, :], 1.0, jnp.where(slot == lpt[1:2, :], 1.0, 0.0)).astype(BF16)
    grp_s[buf] = jnp.dot(onehot, h_ref[...], preferred_element_type=F32).astype(BF16)

    def copies(t, b, act):
        def make_copy(local, glob, size):
            return pltpu.make_async_copy(grp_s.at[b, pl.ds(local, size), :], xs_ref.at[pl.ds(glob, size), :], seg_sem.at[b])

        _segment_copies(t, dst_ref, n_ref, ls_ref, make_copy, act)

    copies(tile, buf, lambda c: c.start())

    @pl.when(tile > 0)
    def _():
        copies(tile - 1, 1 - buf, lambda c: c.wait())

    @pl.when(tile == n_tiles - 1)
    def _():
        copies(tile, buf, lambda c: c.wait())


def _dispatch(h, lp, dst, seg_n, seg_ls, slots, tm):
    n, d = h.shape
    assert tm == TOKEN_TILE
    grid_spec = pltpu.PrefetchScalarGridSpec(
        num_scalar_prefetch=3,
        grid=(n // tm,),
        in_specs=[
            pl.BlockSpec((tm, d), lambda i, *_: (i, 0)),
            pl.BlockSpec((tm, LANES), lambda i, *_: (i, 0)),
        ],
        out_specs=pl.BlockSpec(memory_space=pl.ANY),
        scratch_shapes=[pltpu.VMEM((2, _group_rows(tm), d), BF16), pltpu.VMEM((tm, d), BF16),
                        pltpu.SemaphoreType.DMA(()), pltpu.SemaphoreType.DMA((2,))],
    )
    return pl.pallas_call(
        _dispatch_kernel,
        grid_spec=grid_spec,
        out_shape=jax.ShapeDtypeStruct((slots, d), BF16),
        compiler_params=_params("arbitrary"),
        name="moe_dispatch",
    )(dst, seg_n, seg_ls, h, lp)


def _swiglu_step(h, wg, wu, wd):
    tf = wd.shape[0]
    au = jnp.dot(h, jnp.concatenate([wg, wu], axis=1), preferred_element_type=F32)
    return jnp.dot((_silu(au[:, :tf]) * au[:, tf:]).astype(BF16), wd, preferred_element_type=F32)


def _experts_kernel(te_ref, tr_ref, ts_ref, xs_ref, wg_ref, wu_ref, wd_ref, ys_ref, acc_s):
    del te_ref, ts_ref
    i = pl.program_id(0)
    f = pl.program_id(1)
    rows = tr_ref[i]
    part = xs_ref.shape[0] // EXPERT_TILE_PARTS

    @pl.when(f == 0)
    def _():
        acc_s[...] = jnp.zeros_like(acc_s)

    for k in range(1, EXPERT_TILE_PARTS + 1):
        @pl.when(jnp.logical_and(rows > (k - 1) * part, rows <= k * part))
        def _(used=k * part):
            acc_s[0:used, :] += _swiglu_step(xs_ref[0:used, :], wg_ref[0], wu_ref[0], wd_ref[0])

    @pl.when(f == pl.num_programs(1) - 1)
    def _():
        ys_ref[...] = acc_s[...].astype(BF16)


def _experts(xs, tile_expert, tile_rows, tile_src, wg, wu, wd, tm, tf):
    slots, d = xs.shape
    ff = wd.shape[1]
    nf = ff // tf
    fidx = lambda f, tr, i: jnp.where(tr[i] > 0, f, nf - 1)
    grid_spec = pltpu.PrefetchScalarGridSpec(
        num_scalar_prefetch=3,
        grid=(slots // tm, nf),
        in_specs=[
            pl.BlockSpec((tm, d), lambda i, f, te, tv, ts: (ts[i], 0)),
            pl.BlockSpec((1, d, tf), lambda i, f, te, tv, ts: (te[i], 0, fidx(f, tv, i))),
            pl.BlockSpec((1, d, tf), lambda i, f, te, tv, ts: (te[i], 0, fidx(f, tv, i))),
            pl.BlockSpec((1, tf, d), lambda i, f, te, tv, ts: (te[i], fidx(f, tv, i), 0)),
        ],
        out_specs=pl.BlockSpec((tm, d), lambda i, f, te, tv, ts: (i, 0)),
        scratch_shapes=[pltpu.VMEM((tm, d), F32)],
    )
    return pl.pallas_call(
        _experts_kernel,
        grid_spec=grid_spec,
        out_shape=jax.ShapeDtypeStruct((slots, d), BF16),
        compiler_params=_params("parallel", "arbitrary"),
        name="moe_experts",
    )(tile_expert, tile_rows, tile_src, xs, wg, wu, wd)


def _combine_kernel(dst_ref, n_ref, ls_ref, x_ref, w_ref, lp_ref, mod_ref, gf_ref, ys_ref, out_ref, y_s, sem):
    d = x_ref.shape[-1]
    tm = x_ref.shape[1]
    gb = y_s.shape[1]
    tile = pl.program_id(0) * pl.num_programs(1) + pl.program_id(1)
    n_tiles = pl.num_programs(0) * pl.num_programs(1)
    buf = tile % 2

    def fetch(t, b, act):
        def make_copy(local, glob, size):
            return pltpu.make_async_copy(ys_ref.at[pl.ds(glob, size), :], y_s.at[b, pl.ds(local, size), :], sem.at[b])

        _segment_copies(t, dst_ref, n_ref, ls_ref, make_copy, act)

    def start_fetch(t, b):
        y_s[b, TOP_K * tm:, :] = jnp.zeros((gb - TOP_K * tm, d), BF16)
        fetch(t, b, lambda c: c.start())

    @pl.when(tile == 0)
    def _():
        start_fetch(tile, buf)

    @pl.when(tile + 1 < n_tiles)
    def _():
        start_fetch(tile + 1, 1 - buf)

    fetch(tile, buf, lambda c: c.wait())

    y = y_s[buf]
    lp = lp_ref[0]
    w = w_ref[0]
    slot = lax.broadcasted_iota(jnp.int32, (tm, gb), 1).astype(F32)
    y1 = jnp.dot(jnp.where(slot == lp[:, 0:1], 1.0, 0.0).astype(BF16), y, preferred_element_type=F32)
    y2 = jnp.dot(jnp.where(slot == lp[:, 1:2], 1.0, 0.0).astype(BF16), y, preferred_element_type=F32)
    f = w[:, 0:1] * y1 + w[:, 1:2] * y2
    x = x_ref[0] + mod_ref[0, :, 2 * d:3 * d] * f
    var = jnp.mean(x * x, axis=-1, keepdims=True)
    out_ref[0] = (x * lax.rsqrt(var + RMS_EPS)) * gf_ref[...]


def _combine(x, w, lp, mod, gf, ys, dst, seg_n, seg_ls, tm):
    b, t, d = x.shape
    assert tm == TOKEN_TILE
    tok = lambda i, j, *_: (i, j, 0)
    grid_spec = pltpu.PrefetchScalarGridSpec(
        num_scalar_prefetch=3,
        grid=(b, t // tm),
        in_specs=[
            pl.BlockSpec((1, tm, d), tok),
            pl.BlockSpec((1, tm, LANES), tok),
            pl.BlockSpec((1, tm, LANES), tok),
            pl.BlockSpec((1, 1, 3 * d), lambda i, j, *_: (i, 0, 0)),
            pl.BlockSpec((1, d), lambda i, j, *_: (0, 0)),
            pl.BlockSpec(memory_space=pl.ANY),
        ],
        out_specs=pl.BlockSpec((1, tm, d), tok),
        scratch_shapes=[pltpu.VMEM((2, _group_rows(tm), d), BF16), pltpu.SemaphoreType.DMA((2,))],
    )
    return pl.pallas_call(
        _combine_kernel,
        grid_spec=grid_spec,
        out_shape=jax.ShapeDtypeStruct((b, t, d), F32),
        compiler_params=_params("arbitrary", "arbitrary"),
        name="moe_combine",
    )(dst, seg_n, seg_ls, x, w, lp, mod, gf, ys)


def _trunk(x, mods_mix, mods_ffn, p):
    b, t, d = x.shape
    tm = TOKEN_TILE
    assert t % tm == 0

    q, k, vt = _qkv(x, mods_mix[0], p["ln_mix_g"][0], p["wq"], p["wk"], p["wvt"], tm)
    rows = t // GRID_W
    rpp = 8 if (rows // 2) % 8 == 0 else 1
    o = _na(q, k, vt, p["na_bias"], head_groups=2, row_pairs_per_step=rpp)
    ff = p["ffn_wd"].shape[0]
    x = _proj_ffn(x, o, mods_mix[0], mods_ffn[0], p["ln_ffn_g"][0], p["wo"], p["ffn_wg"], p["ffn_wu"], p["ffn_wd"],
                  tm, _largest_tile(ff, FF_TILE_DENSE))

    x, h, w, lp, seg = _pool_route(x, mods_mix[1], p["ln_mix_g"][1], p["pool_w_in"], p["pool_w_grp"], p["pool_scale"],
                                   p["pool_w_out"], mods_ffn[1], p["ln_ffn_g"][1], p["wr"], p["br"], tm)

    n = b * t
    nt = n // tm
    seg_n = seg[:, 0, :N_EXPERTS]
    seg_ls = seg[:, 1, :N_EXPERTS]
    before = jnp.cumsum(seg_n, axis=0) - seg_n
    total = jnp.sum(seg_n, axis=0)
    te = EXPERT_TILE
    group = ((total + te - 1) // te) * te
    ends = jnp.cumsum(group)
    dst = (ends - group)[None, :] + before
    n_tiles = -(-(TOP_K * n + nt * N_EXPERTS * (BF16_ROWS - 1)) // te) + N_EXPERTS
    tile_start = jnp.arange(n_tiles, dtype=jnp.int32) * te
    tile_live = (tile_start < ends[-1]).astype(jnp.int32)
    tile_expert = jnp.sum((ends[None, :] <= tile_start[:, None]).astype(jnp.int32), axis=1)
    last_expert = jnp.sum((ends <= ends[-1] - 1).astype(jnp.int32))
    tile_expert = jnp.minimum(jnp.where(tile_live > 0, tile_expert, last_expert), N_EXPERTS - 1)
    tile_src = jnp.minimum(jnp.arange(n_tiles, dtype=jnp.int32), jnp.sum(tile_live) - 1)
    filled_end = (ends - group + total)[tile_expert]
    tile_rows = jnp.where(tile_live > 0, jnp.clip(filled_end - tile_start, 0, te), 0).astype(jnp.int32)
    lead = jnp.arange(N_EXPERTS) == 0
    fills = (n_tiles * te - ends[-1]) // tm
    dst = jnp.concatenate([dst, (ends - group + total)[None, :], jnp.where(lead, ends[-1], 0)[None, :]], axis=0)
    seg_n = jnp.concatenate([seg_n, (group - total)[None, :], jnp.where(lead, fills, 0)[None, :]], axis=0)
    seg_ls = jnp.concatenate([seg_ls, jnp.zeros((2, N_EXPERTS), seg_ls.dtype)], axis=0)
    dst, seg_n, seg_ls = (a.reshape(-1).astype(jnp.int32) for a in (dst, seg_n, seg_ls))

    xs = _dispatch(h.reshape(n, d), lp.reshape(n, LANES), dst, seg_n, seg_ls, n_tiles * te, tm)
    ys = _experts(xs, tile_expert, tile_rows, tile_src, p["moe_wg"], p["moe_wu"], p["moe_wd"], te,
                  _largest_tile(ff, FF_TILE_MOE))
    return _combine(x, w, lp, mods_ffn[1], p["ln_f_g"], ys, dst, seg_n, seg_ls, tm)


def kernel(x_prompt, x_sample, c_prompt, c_sample, ln_mix_g, ada_mix_w, ada_mix_b, ln_ffn_g, ada_ffn_w, ada_ffn_b, na_w_qkv, na_rpb, na_w_o, pool_w_in, pool_w_grp, pool_scale, pool_w_out, ffn_w_gate, ffn_w_up, ffn_w_down, moe_w_router, moe_b_router, moe_w_gate, moe_w_up, moe_w_down, ln_f_g):
    d = x_prompt.shape[-1]
    depth = ln_mix_g.shape[0]
    assert depth == 2 and d == NA_HEADS * HEAD_DIM
    bp, bs = c_prompt.shape[0], c_sample.shape[0]

    pad = (-(bp + bs)) % SUBLANES
    c_all = jnp.concatenate([c_prompt, c_sample, jnp.zeros((pad, d), F32)], axis=0)
    mods_mix = _adaln(c_all, ada_mix_w, ada_mix_b)
    mods_ffn = _adaln(c_all, ada_ffn_w, ada_ffn_b)

    wqkv = na_w_qkv[0]
    ne = moe_w_router.shape[-1]
    assert ne == N_EXPERTS
    p = {
        "ln_mix_g": ln_mix_g.reshape(depth, 1, d),
        "ln_ffn_g": ln_ffn_g.reshape(depth, 1, d),
        "ln_f_g": ln_f_g.reshape(1, d),
        "wq": wqkv[:, 0:d].astype(BF16),
        "wk": wqkv[:, d:2 * d].astype(BF16),
        "wvt": wqkv[:, 2 * d:3 * d].T.astype(BF16),
        "na_bias": _na_bias_table(na_rpb[0]),
        "wo": na_w_o[0].astype(BF16),
        "ffn_wg": ffn_w_gate[0].astype(BF16),
        "ffn_wu": ffn_w_up[0].astype(BF16),
        "ffn_wd": ffn_w_down[0].astype(BF16),
        "pool_w_in": pool_w_in[0].astype(BF16),
        "pool_w_grp": pool_w_grp[0].astype(BF16),
        "pool_scale": pool_scale[0].reshape(1, d),
        "pool_w_out": pool_w_out[0].astype(BF16),
        "wr": jnp.pad(moe_w_router[0], ((0, 0), (0, LANES - ne))).astype(BF16),
        "br": jnp.pad(moe_b_router[0].astype(F32), (0, LANES - ne), constant_values=MASK_BIAS).reshape(1, LANES),
        "moe_wg": moe_w_gate[0].astype(BF16),
        "moe_wu": moe_w_up[0].astype(BF16),
        "moe_wd": moe_w_down[0].astype(BF16),
    }

    def group_mods(m, lo, n):
        return m[:, lo:lo + n, None, :]

    y_prompt = _trunk(x_prompt, group_mods(mods_mix, 0, bp), group_mods(mods_ffn, 0, bp), p)
    y_sample = _trunk(x_sample, group_mods(mods_mix, bp, bs), group_mods(mods_ffn, bp, bs), p)
    return (y_prompt, y_sample)
```

```python
import functools

import jax
import jax.numpy as jnp
from jax import lax
from jax.experimental import pallas as pl
from jax.experimental.pallas import tpu as pltpu

F32 = jnp.float32
BF16 = jnp.bfloat16

RMS_EPS = 1e-6
GRID_W = 64
NA_HEADS = 16
HEAD_DIM = 64
NA_KR = 8
NA_KW = 16
POOL_HALF = (1, 2, 4, 8)
N_EXPERTS = 8
TOP_K = 2
LANES = 128
SUBLANES = 8
BF16_ROWS = 16
MASK_BIAS = -1e30
LOG2_E = 1.4426950408889634
ROW_PAIR = 2 * GRID_W
WIN_ROWS = NA_KR + 2
BIAS_TILES = 2 * NA_KR
VMEM_LIMIT = 56 * 1024 * 1024
TOKEN_TILE = 512
FF_TILE_DENSE = 1792
FF_TILE_MOE = 1792
EXPERT_TILE = 512
EXPERT_TILE_PARTS = 4


def _silu(a):
    return a * jax.nn.sigmoid(a)


def _mod_rmsnorm(x, g, shift, scale):
    var = jnp.mean(x * x, axis=-1, keepdims=True)
    y = (x * lax.rsqrt(var + RMS_EPS)) * g
    return y * (1.0 + scale) + shift


def _params(*sem, vmem=VMEM_LIMIT):
    return pltpu.CompilerParams(dimension_semantics=sem, vmem_limit_bytes=vmem)


def _largest_tile(total, target):
    best = LANES
    for cand in range(LANES, min(total, target) + 1, LANES):
        if total % cand == 0:
            best = cand
    assert total % best == 0
    return best


def _adaln_kernel(c_ref, w_ref, b_ref, o_ref):
    s = _silu(c_ref[...]).astype(BF16)
    o_ref[0] = jnp.dot(s, w_ref[0].astype(BF16), preferred_element_type=F32) + b_ref[0]


def _adaln(c_all, w, b):
    depth, d, d3 = w.shape
    r = c_all.shape[0]
    tn = _largest_tile(d3, 1024)
    return pl.pallas_call(
        _adaln_kernel,
        grid=(depth, d3 // tn),
        in_specs=[
            pl.BlockSpec((r, d), lambda i, j: (0, 0)),
            pl.BlockSpec((1, d, tn), lambda i, j: (i, 0, j)),
            pl.BlockSpec((1, 1, tn), lambda i, j: (i, 0, j)),
        ],
        out_specs=pl.BlockSpec((1, r, tn), lambda i, j: (i, 0, j)),
        out_shape=jax.ShapeDtypeStruct((depth, r, d3), F32),
        compiler_params=_params("parallel", "parallel"),
        name="adaln",
    )(c_all, w, b.reshape(depth, 1, d3))


def _qkv_kernel(x_ref, mod_ref, g_ref, wq_ref, wk_ref, wvt_ref, q_ref, k_ref, vt_ref):
    d = x_ref.shape[-1]
    h = _mod_rmsnorm(x_ref[0], g_ref[...], mod_ref[0, :, 0:d], mod_ref[0, :, d:2 * d]).astype(BF16)
    q = jnp.dot(h, wq_ref[...], preferred_element_type=F32) * (HEAD_DIM ** -0.5 * LOG2_E)
    q_ref[0] = q.astype(BF16)
    k_ref[0] = jnp.dot(h, wk_ref[...], preferred_element_type=F32).astype(BF16)
    vt = lax.dot_general(wvt_ref[...], h, (((1,), (1,)), ((), ())), preferred_element_type=F32)
    for j in range(vt_ref.shape[1]):
        vt_ref[0, j] = vt[:, j * LANES:(j + 1) * LANES].astype(BF16)


def _qkv(x, mod, g, wq, wk, wvt, tm):
    b, t, d = x.shape
    const = lambda i, j: (0, 0)
    return pl.pallas_call(
        _qkv_kernel,
        grid=(b, t // tm),
        in_specs=[
            pl.BlockSpec((1, tm, d), lambda i, j: (i, j, 0)),
            pl.BlockSpec((1, 1, 3 * d), lambda i, j: (i, 0, 0)),
            pl.BlockSpec((1, d), const),
            pl.BlockSpec((d, d), const),
            pl.BlockSpec((d, d), const),
            pl.BlockSpec((d, d), const),
        ],
        out_specs=[
            pl.BlockSpec((1, tm, d), lambda i, j: (i, j, 0)),
            pl.BlockSpec((1, tm, d), lambda i, j: (i, j, 0)),
            pl.BlockSpec((1, tm // LANES, d, LANES), lambda i, j: (i, j, 0, 0)),
        ],
        out_shape=[
            jax.ShapeDtypeStruct((b, t, d), BF16),
            jax.ShapeDtypeStruct((b, t, d), BF16),
            jax.ShapeDtypeStruct((b, t // LANES, d, LANES), BF16),
        ],
        compiler_params=_params("parallel", "parallel"),
        name="qkv",
    )(x, mod, g, wq, wk, wvt)


def _na_bias_table(rpb):
    h = rpb.shape[0]
    c = jnp.arange(GRID_W, dtype=jnp.int32)
    cs = jnp.clip(c - NA_KW // 2, 0, GRID_W - NA_KW)
    kc = c[:, None]
    valid = (kc >= cs[None, :]) & (kc < cs[None, :] + NA_KW)
    dc = jnp.clip(kc - c[None, :] + (NA_KW - 1), 0, 2 * NA_KW - 2)
    bt = jnp.where(valid, rpb.astype(F32)[:, :, dc], MASK_BIAS)
    bt = jnp.concatenate([bt, jnp.full((h, 1, GRID_W, GRID_W), MASK_BIAS, F32)], axis=1)
    bt = bt.reshape(h // 2, 2, BIAS_TILES, GRID_W, GRID_W).transpose(0, 2, 3, 1, 4)
    return bt.reshape(h // 2, BIAS_TILES, GRID_W, 2 * GRID_W) * LOG2_E


def _na_kernel(q_ref, k_ref, vt_ref, bias_ref, o_ref, *, rows, pairs_per_step, row_pairs_per_step):
    step = pl.program_id(2)
    lane = lax.broadcasted_iota(jnp.int32, (ROW_PAIR, LANES), 1)
    first_head = lane < HEAD_DIM
    first_head_row = lax.broadcasted_iota(jnp.int32, (GRID_W, LANES), 1) < HEAD_DIM

    def row_pair(j, carry):
        rp = step * row_pairs_per_step + j
        r0 = 2 * rp
        wblk = jnp.clip(rp - NA_KR // 4, 0, (rows - WIN_ROWS) // 2)
        ws = 2 * wblk
        tok = j * ROW_PAIR
        ktok = pl.multiple_of(ws * GRID_W, ROW_PAIR)
        for p in range(pairs_per_step):
            cols = slice(p * LANES, (p + 1) * LANES)
            q2 = q_ref[0, pl.ds(tok, ROW_PAIR), cols]
            qa = jnp.where(first_head, q2, jnp.zeros_like(q2))
            qb = jnp.where(first_head, jnp.zeros_like(q2), q2)
            qbd = jnp.concatenate([qa[:GRID_W], qb[:GRID_W], qa[GRID_W:], qb[GRID_W:]], axis=0)
            kwin = k_ref[0, pl.ds(ktok, WIN_ROWS * GRID_W), cols]
            st = lax.dot_general(kwin, qbd, (((1,), (1,)), ((), ())), preferred_element_type=F32)
            blocks = []
            for i in range(WIN_ROWS):
                kr = ws + i
                tiles = []
                for s in range(2):
                    r = r0 + s
                    rs = jnp.clip(r - NA_KR // 2, 0, rows - NA_KR)
                    inside = jnp.logical_and(kr >= rs, kr < rs + NA_KR)
                    idx = jnp.where(inside, kr - r + (NA_KR - 1), BIAS_TILES - 1)
                    tiles.append(bias_ref[p, idx])
                blocks.append(st[i * GRID_W:(i + 1) * GRID_W] + jnp.concatenate(tiles, axis=1))
            sc = jnp.concatenate(blocks, axis=0)
            m = jnp.max(sc, axis=0, keepdims=True)
            e = jnp.exp2(sc - m)
            l = jnp.sum(e, axis=0, keepdims=True)
            pt = e.astype(BF16)
            ot = jnp.zeros((LANES, 2 * LANES), F32)
            for c in range(WIN_ROWS * GRID_W // LANES):
                vt = vt_ref[0, wblk + c, cols, :]
                ot = ot + jnp.dot(vt, pt[c * LANES:(c + 1) * LANES], preferred_element_type=F32)
            ot = ot * (1.0 / l)
            outs = []
            for s in range(2):
                tt = ot[:, s * LANES:(s + 1) * LANES].T
                outs.append(jnp.where(first_head_row, tt[:GRID_W], tt[GRID_W:]))
            o_ref[0, pl.ds(tok, ROW_PAIR), cols] = jnp.concatenate(outs, axis=0).astype(BF16)
        return carry

    for j in range(row_pairs_per_step):
        row_pair(j, 0)


def _na(q, k, vt, bias, *, head_groups, row_pairs_per_step):
    b, t, d = q.shape
    rows = t // GRID_W
    assert rows % 2 == 0 and rows >= WIN_ROWS and (rows // 2) % row_pairs_per_step == 0
    gc = d // head_groups
    pairs = gc // LANES
    tq = row_pairs_per_step * ROW_PAIR
    kern = functools.partial(_na_kernel, rows=rows, pairs_per_step=pairs, row_pairs_per_step=row_pairs_per_step)
    return pl.pallas_call(
        kern,
        grid=(b, head_groups, t // tq),
        in_specs=[
            pl.BlockSpec((1, tq, gc), lambda i, g, j: (i, j, g)),
            pl.BlockSpec((1, t, gc), lambda i, g, j: (i, 0, g)),
            pl.BlockSpec((1, t // LANES, gc, LANES), lambda i, g, j: (i, 0, g, 0)),
            pl.BlockSpec((pairs, BIAS_TILES, GRID_W, LANES), lambda i, g, j: (g, 0, 0, 0)),
        ],
        out_specs=pl.BlockSpec((1, tq, gc), lambda i, g, j: (i, j, g)),
        out_shape=jax.ShapeDtypeStruct((b, t, d), BF16),
        compiler_params=_params("parallel", "parallel", "parallel"),
        name="na_attn",
    )(q, k, vt, bias)


def _proj_ffn_kernel(x_ref, o_ref, mm_ref, mf_ref, g_ref, wo_ref, wg_ref, wu_ref, wd_ref, out_ref, x1_s, h_s, acc_s):
    d = x_ref.shape[-1]
    f = pl.program_id(2)

    @pl.when(f == 0)
    def _():
        m = jnp.dot(o_ref[0], wo_ref[...], preferred_element_type=F32)
        x1 = x_ref[0] + mm_ref[0, :, 2 * d:3 * d] * m
        x1_s[...] = x1
        h_s[...] = _mod_rmsnorm(x1, g_ref[...], mf_ref[0, :, 0:d], mf_ref[0, :, d:2 * d]).astype(BF16)
        acc_s[...] = jnp.zeros_like(acc_s)

    acc_s[...] += _swiglu_step(h_s[...], wg_ref[...], wu_ref[...], wd_ref[...])

    @pl.when(f == pl.num_programs(2) - 1)
    def _():
        out_ref[0] = x1_s[...] + mf_ref[0, :, 2 * d:3 * d] * acc_s[...]


def _proj_ffn(x, o, mod_mix, mod_ffn, g, wo, wg, wu, wd, tm, tf):
    b, t, d = x.shape
    ff = wd.shape[0]
    tok = lambda i, j, f: (i, j, 0)
    seq = lambda i, j, f: (i, 0, 0)
    return pl.pallas_call(
        _proj_ffn_kernel,
        grid=(b, t // tm, ff // tf),
        in_specs=[
            pl.BlockSpec((1, tm, d), tok),
            pl.BlockSpec((1, tm, d), tok),
            pl.BlockSpec((1, 1, 3 * d), seq),
            pl.BlockSpec((1, 1, 3 * d), seq),
            pl.BlockSpec((1, d), lambda i, j, f: (0, 0)),
            pl.BlockSpec((d, d), lambda i, j, f: (0, 0)),
            pl.BlockSpec((d, tf), lambda i, j, f: (0, f)),
            pl.BlockSpec((d, tf), lambda i, j, f: (0, f)),
            pl.BlockSpec((tf, d), lambda i, j, f: (f, 0)),
        ],
        out_specs=pl.BlockSpec((1, tm, d), tok),
        out_shape=jax.ShapeDtypeStruct((b, t, d), F32),
        scratch_shapes=[pltpu.VMEM((tm, d), F32), pltpu.VMEM((tm, d), BF16), pltpu.VMEM((tm, d), F32)],
        compiler_params=_params("parallel", "parallel", "arbitrary"),
        name="proj_ffn",
    )(x, o, mod_mix, mod_ffn, g, wo, wg, wu, wd)


def _pool_route_kernel(x_ref, xp_ref, xn_ref, mod_ref, g_ref, win_ref, wgrp_ref, ps_ref, wout_ref,
                       modf_ref, gf_ref, wr_ref, br_ref, out_ref, h_ref, w_ref, lp_ref, seg_ref, *, seq_len):
    d = x_ref.shape[-1]
    tm = x_ref.shape[1]
    halo = SUBLANES
    n = tm + 2 * halo
    j = pl.program_id(1)
    x = x_ref[0]
    xe = jnp.concatenate([xp_ref[0], x, xn_ref[0]], axis=0)
    h = _mod_rmsnorm(xe, g_ref[...], mod_ref[0, :, 0:d], mod_ref[0, :, d:2 * d]).astype(BF16)
    u = jnp.dot(h, win_ref[...], preferred_element_type=F32)
    pos = j * tm - halo + lax.broadcasted_iota(jnp.int32, (n, 1), 0)
    u = jnp.where(jnp.logical_and(pos >= 0, pos < seq_len), u, 0.0)
    gc = d // len(POOL_HALF)
    ys = []
    for gi, half in enumerate(POOL_HALF):
        ug = u[:, gi * gc:(gi + 1) * gc]
        fwd = ug
        span = 1
        while span < half:
            fwd = fwd + pltpu.roll(fwd, n - span, axis=0)
            span *= 2
        win = fwd + pltpu.roll(fwd, half, axis=0)
        cnt = jnp.clip(pos + half, 0, seq_len) - jnp.clip(pos - half, 0, seq_len)
        cnt = jnp.maximum(cnt, 1).astype(F32)
        pooled = (win / cnt - ug)[halo:halo + tm].astype(BF16)
        y = jnp.dot(pooled, wgrp_ref[gi], preferred_element_type=F32) * ps_ref[:, gi * gc:(gi + 1) * gc]
        ys.append(y.astype(BF16))
    m = jnp.dot(jnp.concatenate(ys, axis=1), wout_ref[...], preferred_element_type=F32)
    xo = x + mod_ref[0, :, 2 * d:3 * d] * m
    out_ref[0] = xo
    _route(xo, modf_ref, gf_ref, wr_ref, br_ref, h_ref, w_ref, lp_ref, seg_ref)


def _pool_route(x, mod, g, w_in, w_grp, pscale, w_out, mod_ffn, g_ffn, wr, br, tm):
    b, t, d = x.shape
    nt = t // tm
    hb = tm // SUBLANES
    last = t // SUBLANES - 1
    const2 = lambda i, j: (0, 0)
    tok = lambda i, j: (i, j, 0)
    return pl.pallas_call(
        functools.partial(_pool_route_kernel, seq_len=t),
        grid=(b, nt),
        in_specs=[
            pl.BlockSpec((1, tm, d), lambda i, j: (i, j, 0)),
            pl.BlockSpec((1, SUBLANES, d), lambda i, j: (i, jnp.maximum(j * hb - 1, 0), 0)),
            pl.BlockSpec((1, SUBLANES, d), lambda i, j: (i, jnp.minimum((j + 1) * hb, last), 0)),
            pl.BlockSpec((1, 1, 3 * d), lambda i, j: (i, 0, 0)),
            pl.BlockSpec((1, d), const2),
            pl.BlockSpec((d, d), const2),
            pl.BlockSpec(w_grp.shape, lambda i, j: (0, 0, 0)),
            pl.BlockSpec((1, d), const2),
            pl.BlockSpec((d, d), const2),
            pl.BlockSpec((1, 1, 3 * d), lambda i, j: (i, 0, 0)),
            pl.BlockSpec((1, d), const2),
            pl.BlockSpec((d, LANES), const2),
            pl.BlockSpec((1, LANES), const2),
        ],
        out_specs=[
            pl.BlockSpec((1, tm, d), tok),
            pl.BlockSpec((1, tm, d), tok),
            pl.BlockSpec((1, tm, LANES), tok),
            pl.BlockSpec((1, tm, LANES), tok),
            pl.BlockSpec((1, SUBLANES, LANES), lambda i, j: (i * nt + j, 0, 0)),
        ],
        out_shape=[
            jax.ShapeDtypeStruct((b, t, d), F32),
            jax.ShapeDtypeStruct((b, t, d), BF16),
            jax.ShapeDtypeStruct((b, t, LANES), F32),
            jax.ShapeDtypeStruct((b, t, LANES), F32),
            jax.ShapeDtypeStruct((b * nt, SUBLANES, LANES), jnp.int32),
        ],
        compiler_params=_params("parallel", "parallel"),
        name="pool_route",
    )(x, x, x, mod, g, w_in, w_grp, pscale, w_out, mod_ffn, g_ffn, wr, br)


def _group_rows(tm):
    return TOP_K * tm + N_EXPERTS * BF16_ROWS


def _route(x, mod_ref, g_ref, wr_ref, br_ref, h_ref, w_ref, lp_ref, seg_ref):
    tm, d = x.shape
    h = _mod_rmsnorm(x, g_ref[...], mod_ref[0, :, 0:d], mod_ref[0, :, d:2 * d]).astype(BF16)
    h_ref[0] = h
    logits = jnp.dot(h, wr_ref[...], preferred_element_type=F32) + br_ref[...]
    lane = lax.broadcasted_iota(jnp.int32, logits.shape, 1).astype(F32)
    v1 = jnp.max(logits, axis=-1, keepdims=True)
    i1 = jnp.min(jnp.where(logits == v1, lane, float(LANES)), axis=-1, keepdims=True)
    rest = jnp.where(lane == i1, -jnp.inf, logits)
    v2 = jnp.max(rest, axis=-1, keepdims=True)
    i2 = jnp.min(jnp.where(rest == v2, lane, float(LANES)), axis=-1, keepdims=True)
    e2 = jnp.exp(v2 - v1)
    w1 = 1.0 / (1.0 + e2)
    w2 = e2 / (1.0 + e2)
    sel = jnp.where(lane == i1, 1.0, jnp.where(lane == i2, 1.0, 0.0))
    row = lax.broadcasted_iota(jnp.int32, (tm, tm), 0)
    col = lax.broadcasted_iota(jnp.int32, (tm, tm), 1)
    before = jnp.where(col < row, 1.0, 0.0).astype(BF16)
    rank = jnp.dot(before, sel.astype(BF16), preferred_element_type=F32)
    count = jnp.sum(sel, axis=0, keepdims=True)
    tiles = jnp.floor((count + (BF16_ROWS - 1)) * (1.0 / BF16_ROWS))
    er = lax.broadcasted_iota(jnp.int32, (LANES, LANES), 0)
    ec = lax.broadcasted_iota(jnp.int32, (LANES, LANES), 1)
    lower = jnp.where(er < ec, 1.0, 0.0).astype(BF16)
    tiles8 = jnp.broadcast_to(tiles, (SUBLANES, LANES)).astype(BF16)
    start = jnp.dot(tiles8, lower, preferred_element_type=F32)[0:1] * BF16_ROWS
    slot = start + rank
    lp1 = jnp.sum(jnp.where(lane == i1, slot, 0.0), axis=-1, keepdims=True)
    lp2 = jnp.sum(jnp.where(lane == i2, slot, 0.0), axis=-1, keepdims=True)
    w_ref[0] = jnp.where(lane == 0, w1, jnp.where(lane == 1, w2, 0.0))
    lp_ref[0] = jnp.where(lane == 0, lp1, jnp.where(lane == 1, lp2, 0.0))
    sub = lax.broadcasted_iota(jnp.int32, (SUBLANES, LANES), 0)
    seg = jnp.where(sub == 0, tiles * BF16_ROWS, jnp.where(sub == 1, start, 0.0))
    seg_ref[0] = seg.astype(jnp.int32)


def _segment_copies(tile, dst_ref, n_ref, ls_ref, make_copy, act):
    for e in range(N_EXPERTS):
        n = n_ref[tile * N_EXPERTS + e]
        ls = ls_ref[tile * N_EXPERTS + e]
        dst = dst_ref[tile * N_EXPERTS + e]
        size = BF16_ROWS
        while size <= TOKEN_TILE:
            done = n & (-2 * size)

            @pl.when((n & size) != 0)
            def _(size=size, done=done, ls=ls, dst=dst):
                act(make_copy(pl.multiple_of(ls + done, BF16_ROWS), pl.multiple_of(dst + done, BF16_ROWS), size))

            size *= 2


def _dispatch_kernel(dst_ref, n_ref, ls_ref, h_ref, lp_ref, xs_ref, grp_s, zero_s, sem, seg_sem):
    tile = pl.program_id(0)
    n_tiles = pl.num_programs(0)
    tm = h_ref.shape[0]

    @pl.when(tile == 0)
    def _():
        zero_s[...] = jnp.zeros_like(zero_s)

        def make_fill(local, glob, size):
            del local
            return pltpu.make_async_copy(zero_s.at[pl.ds(0, size), :], xs_ref.at[pl.ds(glob, size), :], sem)

        _segment_copies(n_tiles, dst_ref, n_ref, ls_ref, make_fill, lambda c: c.start())
        _segment_copies(n_tiles, dst_ref, n_ref, ls_ref, make_fill, lambda c: c.wait())

        first = dst_ref[(n_tiles + 1) * N_EXPERTS]
        count = n_ref[(n_tiles + 1) * N_EXPERTS]

        def fill_tile(k):
            return make_fill(0, pl.multiple_of(first + k * tm, tm), tm)

        lax.fori_loop(0, count, lambda k, c: (fill_tile(k).start(), c)[1], 0)
        lax.fori_loop(0, count, lambda k, c: (fill_tile(k).wait(), c)[1], 0)

    gb = grp_s.shape[1]
    buf = tile % 2
    lpt = lp_ref[...].T
    slot = lax.broadcasted_iota(jnp.int32, (gb, tm), 0).astype(F32)
    onehot = jnp.where(slot == lpt[0:1, :], 1.0, jnp.where(slot == lpt[1:2, :], 1.0, 0.0)).astype(BF16)
    grp_s[buf] = jnp.dot(onehot, h_ref[...], preferred_element_type=F32).astype(BF16)

    def copies(t, b, act):
        def make_copy(local, glob, size):
            return pltpu.make_async_copy(grp_s.at[b, pl.ds(local, size), :], xs_ref.at[pl.ds(glob, size), :], seg_sem.at[b])

        _segment_copies(t, dst_ref, n_ref, ls_ref, make_copy, act)

    copies(tile, buf, lambda c: c.start())

    @pl.when(tile > 0)
    def _():
        copies(tile - 1, 1 - buf, lambda c: c.wait())

    @pl.when(tile == n_tiles - 1)
    def _():
        copies(tile, buf, lambda c: c.wait())


def _dispatch(h, lp, dst, seg_n, seg_ls, slots, tm):
    n, d = h.shape
    assert tm == TOKEN_TILE
    grid_spec = pltpu.PrefetchScalarGridSpec(
        num_scalar_prefetch=3,
        grid=(n // tm,),
        in_specs=[
            pl.BlockSpec((tm, d), lambda i, *_: (i, 0)),
            pl.BlockSpec((tm, LANES), lambda i, *_: (i, 0)),
        ],
        out_specs=pl.BlockSpec(memory_space=pl.ANY),
        scratch_shapes=[pltpu.VMEM((2, _group_rows(tm), d), BF16), pltpu.VMEM((tm, d), BF16),
                        pltpu.SemaphoreType.DMA(()), pltpu.SemaphoreType.DMA((2,))],
    )
    return pl.pallas_call(
        _dispatch_kernel,
        grid_spec=grid_spec,
        out_shape=jax.ShapeDtypeStruct((slots, d), BF16),
        compiler_params=_params("arbitrary"),
        name="moe_dispatch",
    )(dst, seg_n, seg_ls, h, lp)


def _swiglu_step(h, wg, wu, wd):
    tf = wd.shape[0]
    au = jnp.dot(h, jnp.concatenate([wg, wu], axis=1), preferred_element_type=F32)
    return jnp.dot((_silu(au[:, :tf]) * au[:, tf:]).astype(BF16), wd, preferred_element_type=F32)


def _experts_kernel(te_ref, tr_ref, ts_ref, xs_ref, wg_ref, wu_ref, wd_ref, ys_ref, acc_s):
    del te_ref, ts_ref
    i = pl.program_id(0)
    f = pl.program_id(1)
    rows = tr_ref[i]
    part = xs_ref.shape[0] // EXPERT_TILE_PARTS

    @pl.when(f == 0)
    def _():
        acc_s[...] = jnp.zeros_like(acc_s)

    for k in range(1, EXPERT_TILE_PARTS + 1):
        @pl.when(jnp.logical_and(rows > (k - 1) * part, rows <= k * part))
        def _(used=k * part):
            acc_s[0:used, :] += _swiglu_step(xs_ref[0:used, :], wg_ref[0, 0], wu_ref[0, 0], wd_ref[0])

    @pl.when(f == pl.num_programs(1) - 1)
    def _():
        ys_ref[...] = acc_s[...].astype(BF16)


def _block_major(w, tf):
    e, d, ff = w.shape
    return w.astype(BF16).reshape(e, d, ff // tf, tf).transpose(0, 2, 1, 3)


def _experts(xs, tile_expert, tile_rows, tile_src, wg, wu, wd, tm):
    slots, d = xs.shape
    nf, tf = wg.shape[1], wg.shape[3]
    fidx = lambda f, tr, i: jnp.where(tr[i] > 0, f, nf - 1)
    grid_spec = pltpu.PrefetchScalarGridSpec(
        num_scalar_prefetch=3,
        grid=(slots // tm, nf),
        in_specs=[
            pl.BlockSpec((tm, d), lambda i, f, te, tv, ts: (ts[i], 0)),
            pl.BlockSpec((1, 1, d, tf), lambda i, f, te, tv, ts: (te[i], fidx(f, tv, i), 0, 0)),
            pl.BlockSpec((1, 1, d, tf), lambda i, f, te, tv, ts: (te[i], fidx(f, tv, i), 0, 0)),
            pl.BlockSpec((1, tf, d), lambda i, f, te, tv, ts: (te[i], fidx(f, tv, i), 0)),
        ],
        out_specs=pl.BlockSpec((tm, d), lambda i, f, te, tv, ts: (i, 0)),
        scratch_shapes=[pltpu.VMEM((tm, d), F32)],
    )
    return pl.pallas_call(
        _experts_kernel,
        grid_spec=grid_spec,
        out_shape=jax.ShapeDtypeStruct((slots, d), BF16),
        compiler_params=_params("parallel", "arbitrary"),
        name="moe_experts",
    )(tile_expert, tile_rows, tile_src, xs, wg, wu, wd)


def _combine_kernel(dst_ref, n_ref, ls_ref, x_ref, w_ref, lp_ref, mod_ref, gf_ref, ys_ref, out_ref, y_s, sem):
    d = x_ref.shape[-1]
    tm = x_ref.shape[1]
    gb = y_s.shape[1]
    tile = pl.program_id(0) * pl.num_programs(1) + pl.program_id(1)
    n_tiles = pl.num_programs(0) * pl.num_programs(1)
    buf = tile % 2

    def fetch(t, b, act):
        def make_copy(local, glob, size):
            return pltpu.make_async_copy(ys_ref.at[pl.ds(glob, size), :], y_s.at[b, pl.ds(local, size), :], sem.at[b])

        _segment_copies(t, dst_ref, n_ref, ls_ref, make_copy, act)

    def start_fetch(t, b):
        y_s[b, TOP_K * tm:, :] = jnp.zeros((gb - TOP_K * tm, d), BF16)
        fetch(t, b, lambda c: c.start())

    @pl.when(tile == 0)
    def _():
        start_fetch(tile, buf)

    @pl.when(tile + 1 < n_tiles)
    def _():
        start_fetch(tile + 1, 1 - buf)

    fetch(tile, buf, lambda c: c.wait())

    y = y_s[buf]
    lp = lp_ref[0]
    w = w_ref[0]
    slot = lax.broadcasted_iota(jnp.int32, (tm, gb), 1).astype(F32)
    y1 = jnp.dot(jnp.where(slot == lp[:, 0:1], 1.0, 0.0).astype(BF16), y, preferred_element_type=F32)
    y2 = jnp.dot(jnp.where(slot == lp[:, 1:2], 1.0, 0.0).astype(BF16), y, preferred_element_type=F32)
    f = w[:, 0:1] * y1 + w[:, 1:2] * y2
    x = x_ref[0] + mod_ref[0, :, 2 * d:3 * d] * f
    var = jnp.mean(x * x, axis=-1, keepdims=True)
    out_ref[0] = (x * lax.rsqrt(var + RMS_EPS)) * gf_ref[...]


def _combine(x, w, lp, mod, gf, ys, dst, seg_n, seg_ls, tm):
    b, t, d = x.shape
    assert tm == TOKEN_TILE
    tok = lambda i, j, *_: (i, j, 0)
    grid_spec = pltpu.PrefetchScalarGridSpec(
        num_scalar_prefetch=3,
        grid=(b, t // tm),
        in_specs=[
            pl.BlockSpec((1, tm, d), tok),
            pl.BlockSpec((1, tm, LANES), tok),
            pl.BlockSpec((1, tm, LANES), tok),
            pl.BlockSpec((1, 1, 3 * d), lambda i, j, *_: (i, 0, 0)),
            pl.BlockSpec((1, d), lambda i, j, *_: (0, 0)),
            pl.BlockSpec(memory_space=pl.ANY),
        ],
        out_specs=pl.BlockSpec((1, tm, d), tok),
        scratch_shapes=[pltpu.VMEM((2, _group_rows(tm), d), BF16), pltpu.SemaphoreType.DMA((2,))],
    )
    return pl.pallas_call(
        _combine_kernel,
        grid_spec=grid_spec,
        out_shape=jax.ShapeDtypeStruct((b, t, d), F32),
        compiler_params=_params("arbitrary", "arbitrary"),
        name="moe_combine",
    )(dst, seg_n, seg_ls, x, w, lp, mod, gf, ys)


def _trunk(x, mods_mix, mods_ffn, p):
    b, t, d = x.shape
    tm = TOKEN_TILE
    assert t % tm == 0

    q, k, vt = _qkv(x, mods_mix[0], p["ln_mix_g"][0], p["wq"], p["wk"], p["wvt"], tm)
    rows = t // GRID_W
    rpp = 8 if (rows // 2) % 8 == 0 else 1
    o = _na(q, k, vt, p["na_bias"], head_groups=2, row_pairs_per_step=rpp)
    ff = p["ffn_wd"].shape[0]
    x = _proj_ffn(x, o, mods_mix[0], mods_ffn[0], p["ln_ffn_g"][0], p["wo"], p["ffn_wg"], p["ffn_wu"], p["ffn_wd"],
                  tm, _largest_tile(ff, FF_TILE_DENSE))

    x, h, w, lp, seg = _pool_route(x, mods_mix[1], p["ln_mix_g"][1], p["pool_w_in"], p["pool_w_grp"], p["pool_scale"],
                                   p["pool_w_out"], mods_ffn[1], p["ln_ffn_g"][1], p["wr"], p["br"], tm)

    n = b * t
    nt = n // tm
    seg_n = seg[:, 0, :N_EXPERTS]
    seg_ls = seg[:, 1, :N_EXPERTS]
    before = jnp.cumsum(seg_n, axis=0) - seg_n
    total = jnp.sum(seg_n, axis=0)
    te = EXPERT_TILE
    group = ((total + te - 1) // te) * te
    ends = jnp.cumsum(group)
    dst = (ends - group)[None, :] + before
    n_tiles = -(-(TOP_K * n + nt * N_EXPERTS * (BF16_ROWS - 1)) // te) + N_EXPERTS
    tile_start = jnp.arange(n_tiles, dtype=jnp.int32) * te
    tile_live = (tile_start < ends[-1]).astype(jnp.int32)
    tile_expert = jnp.sum((ends[None, :] <= tile_start[:, None]).astype(jnp.int32), axis=1)
    last_expert = jnp.sum((ends <= ends[-1] - 1).astype(jnp.int32))
    tile_expert = jnp.minimum(jnp.where(tile_live > 0, tile_expert, last_expert), N_EXPERTS - 1)
    tile_src = jnp.minimum(jnp.arange(n_tiles, dtype=jnp.int32), jnp.sum(tile_live) - 1)
    filled_end = (ends - group + total)[tile_expert]
    tile_rows = jnp.where(tile_live > 0, jnp.clip(filled_end - tile_start, 0, te), 0).astype(jnp.int32)
    lead = jnp.arange(N_EXPERTS) == 0
    fills = (n_tiles * te - ends[-1]) // tm
    dst = jnp.concatenate([dst, (ends - group + total)[None, :], jnp.where(lead, ends[-1], 0)[None, :]], axis=0)
    seg_n = jnp.concatenate([seg_n, (group - total)[None, :], jnp.where(lead, fills, 0)[None, :]], axis=0)
    seg_ls = jnp.concatenate([seg_ls, jnp.zeros((2, N_EXPERTS), seg_ls.dtype)], axis=0)
    dst, seg_n, seg_ls = (a.reshape(-1).astype(jnp.int32) for a in (dst, seg_n, seg_ls))

    xs = _dispatch(h.reshape(n, d), lp.reshape(n, LANES), dst, seg_n, seg_ls, n_tiles * te, tm)
    ys = _experts(xs, tile_expert, tile_rows, tile_src, p["moe_wg"], p["moe_wu"], p["moe_wd"], te)
    return _combine(x, w, lp, mods_ffn[1], p["ln_f_g"], ys, dst, seg_n, seg_ls, tm)


def kernel(x_prompt, x_sample, c_prompt, c_sample, ln_mix_g, ada_mix_w, ada_mix_b, ln_ffn_g, ada_ffn_w, ada_ffn_b, na_w_qkv, na_rpb, na_w_o, pool_w_in, pool_w_grp, pool_scale, pool_w_out, ffn_w_gate, ffn_w_up, ffn_w_down, moe_w_router, moe_b_router, moe_w_gate, moe_w_up, moe_w_down, ln_f_g):
    d = x_prompt.shape[-1]
    depth = ln_mix_g.shape[0]
    assert depth == 2 and d == NA_HEADS * HEAD_DIM
    bp, bs = c_prompt.shape[0], c_sample.shape[0]

    pad = (-(bp + bs)) % SUBLANES
    c_all = jnp.concatenate([c_prompt, c_sample, jnp.zeros((pad, d), F32)], axis=0)
    mods_mix = _adaln(c_all, ada_mix_w, ada_mix_b)
    mods_ffn = _adaln(c_all, ada_ffn_w, ada_ffn_b)

    wqkv = na_w_qkv[0]
    tf_moe = _largest_tile(moe_w_gate.shape[-1], FF_TILE_MOE)
    ne = moe_w_router.shape[-1]
    assert ne == N_EXPERTS
    p = {
        "ln_mix_g": ln_mix_g.reshape(depth, 1, d),
        "ln_ffn_g": ln_ffn_g.reshape(depth, 1, d),
        "ln_f_g": ln_f_g.reshape(1, d),
        "wq": wqkv[:, 0:d].astype(BF16),
        "wk": wqkv[:, d:2 * d].astype(BF16),
        "wvt": wqkv[:, 2 * d:3 * d].T.astype(BF16),
        "na_bias": _na_bias_table(na_rpb[0]),
        "wo": na_w_o[0].astype(BF16),
        "ffn_wg": ffn_w_gate[0].astype(BF16),
        "ffn_wu": ffn_w_up[0].astype(BF16),
        "ffn_wd": ffn_w_down[0].astype(BF16),
        "pool_w_in": pool_w_in[0].astype(BF16),
        "pool_w_grp": pool_w_grp[0].astype(BF16),
        "pool_scale": pool_scale[0].reshape(1, d),
        "pool_w_out": pool_w_out[0].astype(BF16),
        "wr": jnp.pad(moe_w_router[0], ((0, 0), (0, LANES - ne))).astype(BF16),
        "br": jnp.pad(moe_b_router[0].astype(F32), (0, LANES - ne), constant_values=MASK_BIAS).reshape(1, LANES),
        "moe_wg": _block_major(moe_w_gate[0], tf_moe),
        "moe_wu": _block_major(moe_w_up[0], tf_moe),
        "moe_wd": moe_w_down[0].astype(BF16),
    }

    def group_mods(m, lo, n):
        return m[:, lo:lo + n, None, :]

    y_prompt = _trunk(x_prompt, group_mods(mods_mix, 0, bp), group_mods(mods_ffn, 0, bp), p)
    y_sample = _trunk(x_sample, group_mods(mods_mix, bp, bs), group_mods(mods_ffn, bp, bs), p)
    return (y_prompt, y_sample)
```

```python
import functools

import jax
import jax.numpy as jnp
from jax import lax
from jax.experimental import pallas as pl
from jax.experimental.pallas import tpu as pltpu

F32 = jnp.float32
BF16 = jnp.bfloat16

RMS_EPS = 1e-6
GRID_W = 64
NA_HEADS = 16
HEAD_DIM = 64
NA_KR = 8
NA_KW = 16
POOL_HALF = (1, 2, 4, 8)
N_EXPERTS = 8
TOP_K = 2
LANES = 128
SUBLANES = 8
BF16_ROWS = 16
MASK_BIAS = -1e30
LOG2_E = 1.4426950408889634
ROW_PAIR = 2 * GRID_W
WIN_ROWS = NA_KR + 2
BIAS_TILES = 2 * NA_KR
VMEM_LIMIT = 56 * 1024 * 1024
TOKEN_TILE = 512
FF_TILE_DENSE = 1792
FF_TILE_MOE = 1792
EXPERT_TILE = 512
EXPERT_TILE_PARTS = 4


def _silu(a):
    return a * jax.nn.sigmoid(a)


def _mod_rmsnorm(x, g, shift, scale):
    var = jnp.mean(x * x, axis=-1, keepdims=True)
    y = (x * lax.rsqrt(var + RMS_EPS)) * g
    return y * (1.0 + scale) + shift


def _params(*sem, vmem=VMEM_LIMIT):
    return pltpu.CompilerParams(dimension_semantics=sem, vmem_limit_bytes=vmem)


def _largest_tile(total, target):
    best = LANES
    for cand in range(LANES, min(total, target) + 1, LANES):
        if total % cand == 0:
            best = cand
    assert total % best == 0
    return best


def _adaln_kernel(c_ref, w_ref, b_ref, o_ref):
    s = _silu(c_ref[...]).astype(BF16)
    o_ref[0] = jnp.dot(s, w_ref[0].astype(BF16), preferred_element_type=F32) + b_ref[0]


def _adaln(c_all, w, b):
    depth, d, d3 = w.shape
    r = c_all.shape[0]
    tn = _largest_tile(d3, 1024)
    return pl.pallas_call(
        _adaln_kernel,
        grid=(depth, d3 // tn),
        in_specs=[
            pl.BlockSpec((r, d), lambda i, j: (0, 0)),
            pl.BlockSpec((1, d, tn), lambda i, j: (i, 0, j)),
            pl.BlockSpec((1, 1, tn), lambda i, j: (i, 0, j)),
        ],
        out_specs=pl.BlockSpec((1, r, tn), lambda i, j: (i, 0, j)),
        out_shape=jax.ShapeDtypeStruct((depth, r, d3), F32),
        compiler_params=_params("parallel", "parallel"),
        name="adaln",
    )(c_all, w, b.reshape(depth, 1, d3))


def _qkv_kernel(x_ref, mod_ref, g_ref, wq_ref, wk_ref, wvt_ref, q_ref, k_ref, vt_ref):
    d = x_ref.shape[-1]
    h = _mod_rmsnorm(x_ref[0], g_ref[...], mod_ref[0, :, 0:d], mod_ref[0, :, d:2 * d]).astype(BF16)
    q = jnp.dot(h, wq_ref[...], preferred_element_type=F32) * (HEAD_DIM ** -0.5 * LOG2_E)
    q_ref[0] = q.astype(BF16)
    k_ref[0] = jnp.dot(h, wk_ref[...], preferred_element_type=F32).astype(BF16)
    vt = lax.dot_general(wvt_ref[...], h, (((1,), (1,)), ((), ())), preferred_element_type=F32)
    for j in range(vt_ref.shape[1]):
        vt_ref[0, j] = vt[:, j * LANES:(j + 1) * LANES].astype(BF16)


def _qkv(x, mod, g, wq, wk, wvt, tm):
    b, t, d = x.shape
    const = lambda i, j: (0, 0)
    return pl.pallas_call(
        _qkv_kernel,
        grid=(b, t // tm),
        in_specs=[
            pl.BlockSpec((1, tm, d), lambda i, j: (i, j, 0)),
            pl.BlockSpec((1, 1, 3 * d), lambda i, j: (i, 0, 0)),
            pl.BlockSpec((1, d), const),
            pl.BlockSpec((d, d), const),
            pl.BlockSpec((d, d), const),
            pl.BlockSpec((d, d), const),
        ],
        out_specs=[
            pl.BlockSpec((1, tm, d), lambda i, j: (i, j, 0)),
            pl.BlockSpec((1, tm, d), lambda i, j: (i, j, 0)),
            pl.BlockSpec((1, tm // LANES, d, LANES), lambda i, j: (i, j, 0, 0)),
        ],
        out_shape=[
            jax.ShapeDtypeStruct((b, t, d), BF16),
            jax.ShapeDtypeStruct((b, t, d), BF16),
            jax.ShapeDtypeStruct((b, t // LANES, d, LANES), BF16),
        ],
        compiler_params=_params("parallel", "parallel"),
        name="qkv",
    )(x, mod, g, wq, wk, wvt)


def _na_bias_table(rpb):
    h = rpb.shape[0]
    c = jnp.arange(GRID_W, dtype=jnp.int32)
    cs = jnp.clip(c - NA_KW // 2, 0, GRID_W - NA_KW)
    kc = c[:, None]
    valid = (kc >= cs[None, :]) & (kc < cs[None, :] + NA_KW)
    dc = jnp.clip(kc - c[None, :] + (NA_KW - 1), 0, 2 * NA_KW - 2)
    bt = jnp.where(valid, rpb.astype(F32)[:, :, dc], MASK_BIAS)
    bt = jnp.concatenate([bt, jnp.full((h, 1, GRID_W, GRID_W), MASK_BIAS, F32)], axis=1)
    bt = bt.reshape(h // 2, 2, BIAS_TILES, GRID_W, GRID_W).transpose(0, 2, 3, 1, 4)
    return bt.reshape(h // 2, BIAS_TILES, GRID_W, 2 * GRID_W) * LOG2_E


def _na_kernel(q_ref, k_ref, vt_ref, bias_ref, o_ref, *, rows, pairs_per_step, row_pairs_per_step):
    step = pl.program_id(2)
    lane = lax.broadcasted_iota(jnp.int32, (ROW_PAIR, LANES), 1)
    first_head = lane < HEAD_DIM
    first_head_row = lax.broadcasted_iota(jnp.int32, (GRID_W, LANES), 1) < HEAD_DIM

    def row_pair(j, carry):
        rp = step * row_pairs_per_step + j
        r0 = 2 * rp
        wblk = jnp.clip(rp - NA_KR // 4, 0, (rows - WIN_ROWS) // 2)
        ws = 2 * wblk
        tok = j * ROW_PAIR
        ktok = pl.multiple_of(ws * GRID_W, ROW_PAIR)
        for p in range(pairs_per_step):
            cols = slice(p * LANES, (p + 1) * LANES)
            q2 = q_ref[0, pl.ds(tok, ROW_PAIR), cols]
            qa = jnp.where(first_head, q2, jnp.zeros_like(q2))
            qb = jnp.where(first_head, jnp.zeros_like(q2), q2)
            qbd = jnp.concatenate([qa[:GRID_W], qb[:GRID_W], qa[GRID_W:], qb[GRID_W:]], axis=0)
            kwin = k_ref[0, pl.ds(ktok, WIN_ROWS * GRID_W), cols]
            st = lax.dot_general(kwin, qbd, (((1,), (1,)), ((), ())), preferred_element_type=F32)
            blocks = []
            for i in range(WIN_ROWS):
                kr = ws + i
                tiles = []
                for s in range(2):
                    r = r0 + s
                    rs = jnp.clip(r - NA_KR // 2, 0, rows - NA_KR)
                    inside = jnp.logical_and(kr >= rs, kr < rs + NA_KR)
                    idx = jnp.where(inside, kr - r + (NA_KR - 1), BIAS_TILES - 1)
                    tiles.append(bias_ref[p, idx])
                blocks.append(st[i * GRID_W:(i + 1) * GRID_W] + jnp.concatenate(tiles, axis=1))
            sc = jnp.concatenate(blocks, axis=0)
            m = jnp.max(sc, axis=0, keepdims=True)
            e = jnp.exp2(sc - m)
            l = jnp.sum(e, axis=0, keepdims=True)
            pt = e.astype(BF16)
            ot = jnp.zeros((LANES, 2 * LANES), F32)
            for c in range(WIN_ROWS * GRID_W // LANES):
                vt = vt_ref[0, wblk + c, cols, :]
                ot = ot + jnp.dot(vt, pt[c * LANES:(c + 1) * LANES], preferred_element_type=F32)
            ot = ot * (1.0 / l)
            outs = []
            for s in range(2):
                tt = ot[:, s * LANES:(s + 1) * LANES].T
                outs.append(jnp.where(first_head_row, tt[:GRID_W], tt[GRID_W:]))
            o_ref[0, pl.ds(tok, ROW_PAIR), cols] = jnp.concatenate(outs, axis=0).astype(BF16)
        return carry

    for j in range(row_pairs_per_step):
        row_pair(j, 0)


def _na(q, k, vt, bias, *, head_groups, row_pairs_per_step):
    b, t, d = q.shape
    rows = t // GRID_W
    assert rows % 2 == 0 and rows >= WIN_ROWS and (rows // 2) % row_pairs_per_step == 0
    gc = d // head_groups
    pairs = gc // LANES
    tq = row_pairs_per_step * ROW_PAIR
    kern = functools.partial(_na_kernel, rows=rows, pairs_per_step=pairs, row_pairs_per_step=row_pairs_per_step)
    return pl.pallas_call(
        kern,
        grid=(b, head_groups, t // tq),
        in_specs=[
            pl.BlockSpec((1, tq, gc), lambda i, g, j: (i, j, g)),
            pl.BlockSpec((1, t, gc), lambda i, g, j: (i, 0, g)),
            pl.BlockSpec((1, t // LANES, gc, LANES), lambda i, g, j: (i, 0, g, 0)),
            pl.BlockSpec((pairs, BIAS_TILES, GRID_W, LANES), lambda i, g, j: (g, 0, 0, 0)),
        ],
        out_specs=pl.BlockSpec((1, tq, gc), lambda i, g, j: (i, j, g)),
        out_shape=jax.ShapeDtypeStruct((b, t, d), BF16),
        compiler_params=_params("parallel", "parallel", "parallel"),
        name="na_attn",
    )(q, k, vt, bias)


def _proj_ffn_kernel(x_ref, o_ref, mm_ref, mf_ref, g_ref, wo_ref, wg_ref, wu_ref, wd_ref, out_ref, *, tf):
    d = x_ref.shape[-1]
    m = jnp.dot(o_ref[0], wo_ref[...], preferred_element_type=F32)
    x1 = x_ref[0] + mm_ref[0, :, 2 * d:3 * d] * m
    h = _mod_rmsnorm(x1, g_ref[...], mf_ref[0, :, 0:d], mf_ref[0, :, d:2 * d]).astype(BF16)
    acc = jnp.zeros_like(x1)
    for c in range(wd_ref.shape[0] // tf):
        blk = slice(c * tf, (c + 1) * tf)
        acc = acc + _swiglu_step(h, wg_ref[:, blk], wu_ref[:, blk], wd_ref[blk, :])
    out_ref[0] = x1 + mf_ref[0, :, 2 * d:3 * d] * acc


def _proj_ffn(x, o, mod_mix, mod_ffn, g, wo, wg, wu, wd, tm, tf):
    b, t, d = x.shape
    ff = wd.shape[0]
    tok = lambda i, j: (i, j, 0)
    seq = lambda i, j: (i, 0, 0)
    const = lambda i, j: (0, 0)
    return pl.pallas_call(
        functools.partial(_proj_ffn_kernel, tf=tf),
        grid=(b, t // tm),
        in_specs=[
            pl.BlockSpec((1, tm, d), tok),
            pl.BlockSpec((1, tm, d), tok),
            pl.BlockSpec((1, 1, 3 * d), seq),
            pl.BlockSpec((1, 1, 3 * d), seq),
            pl.BlockSpec((1, d), const),
            pl.BlockSpec((d, d), const),
            pl.BlockSpec((d, ff), const),
            pl.BlockSpec((d, ff), const),
            pl.BlockSpec((ff, d), const),
        ],
        out_specs=pl.BlockSpec((1, tm, d), tok),
        out_shape=jax.ShapeDtypeStruct((b, t, d), F32),
        compiler_params=_params("parallel", "parallel"),
        name="proj_ffn",
    )(x, o, mod_mix, mod_ffn, g, wo, wg, wu, wd)


def _pool_route_kernel(x_ref, xp_ref, xn_ref, mod_ref, g_ref, win_ref, wgrp_ref, ps_ref, wout_ref,
                       modf_ref, gf_ref, wr_ref, br_ref, out_ref, h_ref, w_ref, lp_ref, seg_ref, *, seq_len):
    d = x_ref.shape[-1]
    tm = x_ref.shape[1]
    halo = SUBLANES
    n = tm + 2 * halo
    j = pl.program_id(1)
    x = x_ref[0]
    xe = jnp.concatenate([xp_ref[0], x, xn_ref[0]], axis=0)
    h = _mod_rmsnorm(xe, g_ref[...], mod_ref[0, :, 0:d], mod_ref[0, :, d:2 * d]).astype(BF16)
    u = jnp.dot(h, win_ref[...], preferred_element_type=F32)
    pos = j * tm - halo + lax.broadcasted_iota(jnp.int32, (n, 1), 0)
    u = jnp.where(jnp.logical_and(pos >= 0, pos < seq_len), u, 0.0)
    gc = d // len(POOL_HALF)
    ys = []
    for gi, half in enumerate(POOL_HALF):
        ug = u[:, gi * gc:(gi + 1) * gc]
        fwd = ug
        span = 1
        while span < half:
            fwd = fwd + pltpu.roll(fwd, n - span, axis=0)
            span *= 2
        win = fwd + pltpu.roll(fwd, half, axis=0)
        cnt = jnp.clip(pos + half, 0, seq_len) - jnp.clip(pos - half, 0, seq_len)
        cnt = jnp.maximum(cnt, 1).astype(F32)
        pooled = (win / cnt - ug)[halo:halo + tm].astype(BF16)
        y = jnp.dot(pooled, wgrp_ref[gi], preferred_element_type=F32) * ps_ref[:, gi * gc:(gi + 1) * gc]
        ys.append(y.astype(BF16))
    m = jnp.dot(jnp.concatenate(ys, axis=1), wout_ref[...], preferred_element_type=F32)
    xo = x + mod_ref[0, :, 2 * d:3 * d] * m
    out_ref[0] = xo
    _route(xo, modf_ref, gf_ref, wr_ref, br_ref, h_ref, w_ref, lp_ref, seg_ref)


def _pool_route(x, mod, g, w_in, w_grp, pscale, w_out, mod_ffn, g_ffn, wr, br, tm):
    b, t, d = x.shape
    nt = t // tm
    hb = tm // SUBLANES
    last = t // SUBLANES - 1
    const2 = lambda i, j: (0, 0)
    tok = lambda i, j: (i, j, 0)
    return pl.pallas_call(
        functools.partial(_pool_route_kernel, seq_len=t),
        grid=(b, nt),
        in_specs=[
            pl.BlockSpec((1, tm, d), lambda i, j: (i, j, 0)),
            pl.BlockSpec((1, SUBLANES, d), lambda i, j: (i, jnp.maximum(j * hb - 1, 0), 0)),
            pl.BlockSpec((1, SUBLANES, d), lambda i, j: (i, jnp.minimum((j + 1) * hb, last), 0)),
            pl.BlockSpec((1, 1, 3 * d), lambda i, j: (i, 0, 0)),
            pl.BlockSpec((1, d), const2),
            pl.BlockSpec((d, d), const2),
            pl.BlockSpec(w_grp.shape, lambda i, j: (0, 0, 0)),
            pl.BlockSpec((1, d), const2),
            pl.BlockSpec((d, d), const2),
            pl.BlockSpec((1, 1, 3 * d), lambda i, j: (i, 0, 0)),
            pl.BlockSpec((1, d), const2),
            pl.BlockSpec((d, LANES), const2),
            pl.BlockSpec((1, LANES), const2),
        ],
        out_specs=[
            pl.BlockSpec((1, tm, d), tok),
            pl.BlockSpec((1, tm, d), tok),
            pl.BlockSpec((1, tm, LANES), tok),
            pl.BlockSpec((1, tm, LANES), tok),
            pl.BlockSpec((1, SUBLANES, LANES), lambda i, j: (i * nt + j, 0, 0)),
        ],
        out_shape=[
            jax.ShapeDtypeStruct((b, t, d), F32),
            jax.ShapeDtypeStruct((b, t, d), BF16),
            jax.ShapeDtypeStruct((b, t, LANES), F32),
            jax.ShapeDtypeStruct((b, t, LANES), F32),
            jax.ShapeDtypeStruct((b * nt, SUBLANES, LANES), jnp.int32),
        ],
        compiler_params=_params("parallel", "parallel"),
        name="pool_route",
    )(x, x, x, mod, g, w_in, w_grp, pscale, w_out, mod_ffn, g_ffn, wr, br)


def _group_rows(tm):
    return TOP_K * tm + N_EXPERTS * BF16_ROWS


def _route(x, mod_ref, g_ref, wr_ref, br_ref, h_ref, w_ref, lp_ref, seg_ref):
    tm, d = x.shape
    h = _mod_rmsnorm(x, g_ref[...], mod_ref[0, :, 0:d], mod_ref[0, :, d:2 * d]).astype(BF16)
    h_ref[0] = h
    logits = jnp.dot(h, wr_ref[...], preferred_element_type=F32) + br_ref[...]
    lane = lax.broadcasted_iota(jnp.int32, logits.shape, 1).astype(F32)
    v1 = jnp.max(logits, axis=-1, keepdims=True)
    i1 = jnp.min(jnp.where(logits == v1, lane, float(LANES)), axis=-1, keepdims=True)
    rest = jnp.where(lane == i1, -jnp.inf, logits)
    v2 = jnp.max(rest, axis=-1, keepdims=True)
    i2 = jnp.min(jnp.where(rest == v2, lane, float(LANES)), axis=-1, keepdims=True)
    e2 = jnp.exp(v2 - v1)
    w1 = 1.0 / (1.0 + e2)
    w2 = e2 / (1.0 + e2)
    sel = jnp.where(lane == i1, 1.0, jnp.where(lane == i2, 1.0, 0.0))
    row = lax.broadcasted_iota(jnp.int32, (tm, tm), 0)
    col = lax.broadcasted_iota(jnp.int32, (tm, tm), 1)
    before = jnp.where(col < row, 1.0, 0.0).astype(BF16)
    rank = jnp.dot(before, sel.astype(BF16), preferred_element_type=F32)
    count = jnp.sum(sel, axis=0, keepdims=True)
    tiles = jnp.floor((count + (BF16_ROWS - 1)) * (1.0 / BF16_ROWS))
    er = lax.broadcasted_iota(jnp.int32, (LANES, LANES), 0)
    ec = lax.broadcasted_iota(jnp.int32, (LANES, LANES), 1)
    lower = jnp.where(er < ec, 1.0, 0.0).astype(BF16)
    tiles8 = jnp.broadcast_to(tiles, (SUBLANES, LANES)).astype(BF16)
    start = jnp.dot(tiles8, lower, preferred_element_type=F32)[0:1] * BF16_ROWS
    slot = start + rank
    lp1 = jnp.sum(jnp.where(lane == i1, slot, 0.0), axis=-1, keepdims=True)
    lp2 = jnp.sum(jnp.where(lane == i2, slot, 0.0), axis=-1, keepdims=True)
    w_ref[0] = jnp.where(lane == 0, w1, jnp.where(lane == 1, w2, 0.0))
    lp_ref[0] = jnp.where(lane == 0, lp1, jnp.where(lane == 1, lp2, 0.0))
    sub = lax.broadcasted_iota(jnp.int32, (SUBLANES, LANES), 0)
    seg = jnp.where(sub == 0, tiles * BF16_ROWS, jnp.where(sub == 1, start, 0.0))
    seg_ref[0] = seg.astype(jnp.int32)


def _segment_copies(tile, dst_ref, n_ref, ls_ref, make_copy, act):
    for e in range(N_EXPERTS):
        n = n_ref[tile * N_EXPERTS + e]
        ls = ls_ref[tile * N_EXPERTS + e]
        dst = dst_ref[tile * N_EXPERTS + e]
        size = BF16_ROWS
        while size <= TOKEN_TILE:
            done = n & (-2 * size)

            @pl.when((n & size) != 0)
            def _(size=size, done=done, ls=ls, dst=dst):
                act(make_copy(pl.multiple_of(ls + done, BF16_ROWS), pl.multiple_of(dst + done, BF16_ROWS), size))

            size *= 2


def _dispatch_kernel(dst_ref, n_ref, ls_ref, h_ref, lp_ref, xs_ref, grp_s, zero_s, sem, seg_sem):
    tile = pl.program_id(0)
    n_tiles = pl.num_programs(0)
    tm = h_ref.shape[0]

    @pl.when(tile == 0)
    def _():
        zero_s[...] = jnp.zeros_like(zero_s)

        def make_fill(local, glob, size):
            del local
            return pltpu.make_async_copy(zero_s.at[pl.ds(0, size), :], xs_ref.at[pl.ds(glob, size), :], sem)

        _segment_copies(n_tiles, dst_ref, n_ref, ls_ref, make_fill, lambda c: c.start())
        _segment_copies(n_tiles, dst_ref, n_ref, ls_ref, make_fill, lambda c: c.wait())

        first = dst_ref[(n_tiles + 1) * N_EXPERTS]
        count = n_ref[(n_tiles + 1) * N_EXPERTS]

        def fill_tile(k):
            return make_fill(0, pl.multiple_of(first + k * tm, tm), tm)

        lax.fori_loop(0, count, lambda k, c: (fill_tile(k).start(), c)[1], 0)
        lax.fori_loop(0, count, lambda k, c: (fill_tile(k).wait(), c)[1], 0)

    gb = grp_s.shape[1]
    buf = tile % 2
    lpt = lp_ref[...].T
    slot = lax.broadcasted_iota(jnp.int32, (gb, tm), 0).astype(F32)
    onehot = jnp.where(slot == lpt[0:1, :], 1.0, jnp.where(slot == lpt[1:2, :], 1.0, 0.0)).astype(BF16)
    grp_s[buf] = jnp.dot(onehot, h_ref[...], preferred_element_type=F32).astype(BF16)

    def copies(t, b, act):
        def make_copy(local, glob, size):
            return pltpu.make_async_copy(grp_s.at[b, pl.ds(local, size), :], xs_ref.at[pl.ds(glob, size), :], seg_sem.at[b])

        _segment_copies(t, dst_ref, n_ref, ls_ref, make_copy, act)

    copies(tile, buf, lambda c: c.start())

    @pl.when(tile > 0)
    def _():
        copies(tile - 1, 1 - buf, lambda c: c.wait())

    @pl.when(tile == n_tiles - 1)
    def _():
        copies(tile, buf, lambda c: c.wait())


def _dispatch(h, lp, dst, seg_n, seg_ls, slots, tm):
    n, d = h.shape
    assert tm == TOKEN_TILE
    grid_spec = pltpu.PrefetchScalarGridSpec(
        num_scalar_prefetch=3,
        grid=(n // tm,),
        in_specs=[
            pl.BlockSpec((tm, d), lambda i, *_: (i, 0)),
            pl.BlockSpec((tm, LANES), lambda i, *_: (i, 0)),
        ],
        out_specs=pl.BlockSpec(memory_space=pl.ANY),
        scratch_shapes=[pltpu.VMEM((2, _group_rows(tm), d), BF16), pltpu.VMEM((tm, d), BF16),
                        pltpu.SemaphoreType.DMA(()), pltpu.SemaphoreType.DMA((2,))],
    )
    return pl.pallas_call(
        _dispatch_kernel,
        grid_spec=grid_spec,
        out_shape=jax.ShapeDtypeStruct((slots, d), BF16),
        compiler_params=_params("arbitrary"),
        name="moe_dispatch",
    )(dst, seg_n, seg_ls, h, lp)


def _swiglu_step(h, wg, wu, wd):
    tf = wd.shape[0]
    au = jnp.dot(h, jnp.concatenate([wg, wu], axis=1), preferred_element_type=F32)
    return jnp.dot((_silu(au[:, :tf]) * au[:, tf:]).astype(BF16), wd, preferred_element_type=F32)


def _experts_kernel(te_ref, tr_ref, ts_ref, xs_ref, wg_ref, wu_ref, wd_ref, ys_ref, acc_s):
    del te_ref, ts_ref
    i = pl.program_id(0)
    f = pl.program_id(1)
    rows = tr_ref[i]
    part = xs_ref.shape[0] // EXPERT_TILE_PARTS

    @pl.when(f == 0)
    def _():
        acc_s[...] = jnp.zeros_like(acc_s)

    for k in range(1, EXPERT_TILE_PARTS + 1):
        @pl.when(jnp.logical_and(rows > (k - 1) * part, rows <= k * part))
        def _(used=k * part):
            acc_s[0:used, :] += _swiglu_step(xs_ref[0:used, :], wg_ref[0], wu_ref[0], wd_ref[0])

    @pl.when(f == pl.num_programs(1) - 1)
    def _():
        ys_ref[...] = acc_s[...].astype(BF16)


def _experts(xs, tile_expert, tile_rows, tile_src, wg, wu, wd, tm, tf):
    slots, d = xs.shape
    nf = wd.shape[1] // tf
    fidx = lambda f, tr, i: jnp.where(tr[i] > 0, f, nf - 1)
    grid_spec = pltpu.PrefetchScalarGridSpec(
        num_scalar_prefetch=3,
        grid=(slots // tm, nf),
        in_specs=[
            pl.BlockSpec((tm, d), lambda i, f, te, tv, ts: (ts[i], 0)),
            pl.BlockSpec((1, d, tf), lambda i, f, te, tv, ts: (te[i], 0, fidx(f, tv, i))),
            pl.BlockSpec((1, d, tf), lambda i, f, te, tv, ts: (te[i], 0, fidx(f, tv, i))),
            pl.BlockSpec((1, tf, d), lambda i, f, te, tv, ts: (te[i], fidx(f, tv, i), 0)),
        ],
        out_specs=pl.BlockSpec((tm, d), lambda i, f, te, tv, ts: (i, 0)),
        scratch_shapes=[pltpu.VMEM((tm, d), F32)],
    )
    return pl.pallas_call(
        _experts_kernel,
        grid_spec=grid_spec,
        out_shape=jax.ShapeDtypeStruct((slots, d), BF16),
        compiler_params=_params("parallel", "arbitrary"),
        name="moe_experts",
    )(tile_expert, tile_rows, tile_src, xs, wg, wu, wd)


def _combine_kernel(dst_ref, n_ref, ls_ref, x_ref, w_ref, lp_ref, mod_ref, gf_ref, ys_ref, out_ref, y_s, sem):
    d = x_ref.shape[-1]
    tm = x_ref.shape[1]
    gb = y_s.shape[1]
    tile = pl.program_id(0) * pl.num_programs(1) + pl.program_id(1)
    n_tiles = pl.num_programs(0) * pl.num_programs(1)
    buf = tile % 2

    def fetch(t, b, act):
        def make_copy(local, glob, size):
            return pltpu.make_async_copy(ys_ref.at[pl.ds(glob, size), :], y_s.at[b, pl.ds(local, size), :], sem.at[b])

        _segment_copies(t, dst_ref, n_ref, ls_ref, make_copy, act)

    def start_fetch(t, b):
        y_s[b, TOP_K * tm:, :] = jnp.zeros((gb - TOP_K * tm, d), BF16)
        fetch(t, b, lambda c: c.start())

    @pl.when(tile == 0)
    def _():
        start_fetch(tile, buf)

    @pl.when(tile + 1 < n_tiles)
    def _():
        start_fetch(tile + 1, 1 - buf)

    fetch(tile, buf, lambda c: c.wait())

    y = y_s[buf]
    lp = lp_ref[0]
    w = w_ref[0]
    slot = lax.broadcasted_iota(jnp.int32, (tm, gb), 1).astype(F32)
    y1 = jnp.dot(jnp.where(slot == lp[:, 0:1], 1.0, 0.0).astype(BF16), y, preferred_element_type=F32)
    y2 = jnp.dot(jnp.where(slot == lp[:, 1:2], 1.0, 0.0).astype(BF16), y, preferred_element_type=F32)
    f = w[:, 0:1] * y1 + w[:, 1:2] * y2
    x = x_ref[0] + mod_ref[0, :, 2 * d:3 * d] * f
    var = jnp.mean(x * x, axis=-1, keepdims=True)
    out_ref[0] = (x * lax.rsqrt(var + RMS_EPS)) * gf_ref[...]


def _combine(x, w, lp, mod, gf, ys, dst, seg_n, seg_ls, tm):
    b, t, d = x.shape
    assert tm == TOKEN_TILE
    tok = lambda i, j, *_: (i, j, 0)
    grid_spec = pltpu.PrefetchScalarGridSpec(
        num_scalar_prefetch=3,
        grid=(b, t // tm),
        in_specs=[
            pl.BlockSpec((1, tm, d), tok),
            pl.BlockSpec((1, tm, LANES), tok),
            pl.BlockSpec((1, tm, LANES), tok),
            pl.BlockSpec((1, 1, 3 * d), lambda i, j, *_: (i, 0, 0)),
            pl.BlockSpec((1, d), lambda i, j, *_: (0, 0)),
            pl.BlockSpec(memory_space=pl.ANY),
        ],
        out_specs=pl.BlockSpec((1, tm, d), tok),
        scratch_shapes=[pltpu.VMEM((2, _group_rows(tm), d), BF16), pltpu.SemaphoreType.DMA((2,))],
    )
    return pl.pallas_call(
        _combine_kernel,
        grid_spec=grid_spec,
        out_shape=jax.ShapeDtypeStruct((b, t, d), F32),
        compiler_params=_params("arbitrary", "arbitrary"),
        name="moe_combine",
    )(dst, seg_n, seg_ls, x, w, lp, mod, gf, ys)


def _trunk(x, mods_mix, mods_ffn, p):
    b, t, d = x.shape
    tm = TOKEN_TILE
    assert t % tm == 0

    q, k, vt = _qkv(x, mods_mix[0], p["ln_mix_g"][0], p["wq"], p["wk"], p["wvt"], tm)
    rows = t // GRID_W
    rpp = 8 if (rows // 2) % 8 == 0 else 1
    o = _na(q, k, vt, p["na_bias"], head_groups=2, row_pairs_per_step=rpp)
    ff = p["ffn_wd"].shape[0]
    x = _proj_ffn(x, o, mods_mix[0], mods_ffn[0], p["ln_ffn_g"][0], p["wo"], p["ffn_wg"], p["ffn_wu"], p["ffn_wd"],
                  tm, _largest_tile(ff, FF_TILE_DENSE))

    x, h, w, lp, seg = _pool_route(x, mods_mix[1], p["ln_mix_g"][1], p["pool_w_in"], p["pool_w_grp"], p["pool_scale"],
                                   p["pool_w_out"], mods_ffn[1], p["ln_ffn_g"][1], p["wr"], p["br"], tm)

    n = b * t
    nt = n // tm
    seg_n = seg[:, 0, :N_EXPERTS]
    seg_ls = seg[:, 1, :N_EXPERTS]
    before = jnp.cumsum(seg_n, axis=0) - seg_n
    total = jnp.sum(seg_n, axis=0)
    te = EXPERT_TILE
    group = ((total + te - 1) // te) * te
    ends = jnp.cumsum(group)
    dst = (ends - group)[None, :] + before
    n_tiles = -(-(TOP_K * n + nt * N_EXPERTS * (BF16_ROWS - 1)) // te) + N_EXPERTS
    tile_start = jnp.arange(n_tiles, dtype=jnp.int32) * te
    tile_live = (tile_start < ends[-1]).astype(jnp.int32)
    tile_expert = jnp.sum((ends[None, :] <= tile_start[:, None]).astype(jnp.int32), axis=1)
    last_expert = jnp.sum((ends <= ends[-1] - 1).astype(jnp.int32))
    tile_expert = jnp.minimum(jnp.where(tile_live > 0, tile_expert, last_expert), N_EXPERTS - 1)
    tile_src = jnp.minimum(jnp.arange(n_tiles, dtype=jnp.int32), jnp.sum(tile_live) - 1)
    filled_end = (ends - group + total)[tile_expert]
    tile_rows = jnp.where(tile_live > 0, jnp.clip(filled_end - tile_start, 0, te), 0).astype(jnp.int32)
    lead = jnp.arange(N_EXPERTS) == 0
    fills = (n_tiles * te - ends[-1]) // tm
    dst = jnp.concatenate([dst, (ends - group + total)[None, :], jnp.where(lead, ends[-1], 0)[None, :]], axis=0)
    seg_n = jnp.concatenate([seg_n, (group - total)[None, :], jnp.where(lead, fills, 0)[None, :]], axis=0)
    seg_ls = jnp.concatenate([seg_ls, jnp.zeros((2, N_EXPERTS), seg_ls.dtype)], axis=0)
    dst, seg_n, seg_ls = (a.reshape(-1).astype(jnp.int32) for a in (dst, seg_n, seg_ls))

    xs = _dispatch(h.reshape(n, d), lp.reshape(n, LANES), dst, seg_n, seg_ls, n_tiles * te, tm)
    ys = _experts(xs, tile_expert, tile_rows, tile_src, p["moe_wg"], p["moe_wu"], p["moe_wd"], te,
                  _largest_tile(ff, FF_TILE_MOE))
    return _combine(x, w, lp, mods_ffn[1], p["ln_f_g"], ys, dst, seg_n, seg_ls, tm)


def kernel(x_prompt, x_sample, c_prompt, c_sample, ln_mix_g, ada_mix_w, ada_mix_b, ln_ffn_g, ada_ffn_w, ada_ffn_b, na_w_qkv, na_rpb, na_w_o, pool_w_in, pool_w_grp, pool_scale, pool_w_out, ffn_w_gate, ffn_w_up, ffn_w_down, moe_w_router, moe_b_router, moe_w_gate, moe_w_up, moe_w_down, ln_f_g):
    d = x_prompt.shape[-1]
    depth = ln_mix_g.shape[0]
    assert depth == 2 and d == NA_HEADS * HEAD_DIM
    bp, bs = c_prompt.shape[0], c_sample.shape[0]

    pad = (-(bp + bs)) % SUBLANES
    c_all = jnp.concatenate([c_prompt, c_sample, jnp.zeros((pad, d), F32)], axis=0)
    mods_mix = _adaln(c_all, ada_mix_w, ada_mix_b)
    mods_ffn = _adaln(c_all, ada_ffn_w, ada_ffn_b)

    wqkv = na_w_qkv[0]
    ne = moe_w_router.shape[-1]
    assert ne == N_EXPERTS
    p = {
        "ln_mix_g": ln_mix_g.reshape(depth, 1, d),
        "ln_ffn_g": ln_ffn_g.reshape(depth, 1, d),
        "ln_f_g": ln_f_g.reshape(1, d),
        "wq": wqkv[:, 0:d].astype(BF16),
        "wk": wqkv[:, d:2 * d].astype(BF16),
        "wvt": wqkv[:, 2 * d:3 * d].T.astype(BF16),
        "na_bias": _na_bias_table(na_rpb[0]),
        "wo": na_w_o[0].astype(BF16),
        "ffn_wg": ffn_w_gate[0].astype(BF16),
        "ffn_wu": ffn_w_up[0].astype(BF16),
        "ffn_wd": ffn_w_down[0].astype(BF16),
        "pool_w_in": pool_w_in[0].astype(BF16),
        "pool_w_grp": pool_w_grp[0].astype(BF16),
        "pool_scale": pool_scale[0].reshape(1, d),
        "pool_w_out": pool_w_out[0].astype(BF16),
        "wr": jnp.pad(moe_w_router[0], ((0, 0), (0, LANES - ne))).astype(BF16),
        "br": jnp.pad(moe_b_router[0].astype(F32), (0, LANES - ne), constant_values=MASK_BIAS).reshape(1, LANES),
        "moe_wg": moe_w_gate[0].astype(BF16),
        "moe_wu": moe_w_up[0].astype(BF16),
        "moe_wd": moe_w_down[0].astype(BF16),
    }

    def group_mods(m, lo, n):
        return m[:, lo:lo + n, None, :]

    y_prompt = _trunk(x_prompt, group_mods(mods_mix, 0, bp), group_mods(mods_ffn, 0, bp), p)
    y_sample = _trunk(x_sample, group_mods(mods_mix, bp, bs), group_mods(mods_ffn, bp, bs), p)
    return (y_prompt, y_sample)
```

```python
import functools

import jax
import jax.numpy as jnp
from jax import lax
from jax.experimental import pallas as pl
from jax.experimental.pallas import tpu as pltpu

F32 = jnp.float32
BF16 = jnp.bfloat16

RMS_EPS = 1e-6
GRID_W = 64
NA_HEADS = 16
HEAD_DIM = 64
NA_KR = 8
NA_KW = 16
POOL_HALF = (1, 2, 4, 8)
N_EXPERTS = 8
TOP_K = 2
LANES = 128
SUBLANES = 8
BF16_ROWS = 16
MASK_BIAS = -1e30
LOG2_E = 1.4426950408889634
ROW_PAIR = 2 * GRID_W
WIN_ROWS = NA_KR + 2
BIAS_TILES = 2 * NA_KR
VMEM_LIMIT = 56 * 1024 * 1024
TOKEN_TILE = 512
QKV_TILE = 1024
FF_TILE_DENSE = 1792
FF_TILE_MOE = 1792
EXPERT_TILE = 512
EXPERT_TILE_PARTS = 4


def _silu(a):
    return a * jax.nn.sigmoid(a)


def _mod_rmsnorm(x, g, shift, scale):
    var = jnp.mean(x * x, axis=-1, keepdims=True)
    y = (x * lax.rsqrt(var + RMS_EPS)) * g
    return y * (1.0 + scale) + shift


def _params(*sem, vmem=VMEM_LIMIT):
    return pltpu.CompilerParams(dimension_semantics=sem, vmem_limit_bytes=vmem)


def _largest_tile(total, target):
    best = LANES
    for cand in range(LANES, min(total, target) + 1, LANES):
        if total % cand == 0:
            best = cand
    assert total % best == 0
    return best


def _adaln_kernel(c_ref, w_ref, b_ref, o_ref):
    s = _silu(c_ref[...]).astype(BF16)
    o_ref[0] = jnp.dot(s, w_ref[0].astype(BF16), preferred_element_type=F32) + b_ref[0]


def _adaln(c_all, w, b):
    depth, d, d3 = w.shape
    r = c_all.shape[0]
    tn = _largest_tile(d3, 1024)
    return pl.pallas_call(
        _adaln_kernel,
        grid=(depth, d3 // tn),
        in_specs=[
            pl.BlockSpec((r, d), lambda i, j: (0, 0)),
            pl.BlockSpec((1, d, tn), lambda i, j: (i, 0, j)),
            pl.BlockSpec((1, 1, tn), lambda i, j: (i, 0, j)),
        ],
        out_specs=pl.BlockSpec((1, r, tn), lambda i, j: (i, 0, j)),
        out_shape=jax.ShapeDtypeStruct((depth, r, d3), F32),
        compiler_params=_params("parallel", "parallel"),
        name="adaln",
    )(c_all, w, b.reshape(depth, 1, d3))


def _qkv_kernel(x_ref, mod_ref, g_ref, wq_ref, wk_ref, wvt_ref, q_ref, k_ref, vt_ref):
    d = x_ref.shape[-1]
    h = _mod_rmsnorm(x_ref[0], g_ref[...], mod_ref[0, :, 0:d], mod_ref[0, :, d:2 * d]).astype(BF16)
    q = jnp.dot(h, wq_ref[...], preferred_element_type=F32) * (HEAD_DIM ** -0.5 * LOG2_E)
    q_ref[0] = q.astype(BF16)
    k_ref[0] = jnp.dot(h, wk_ref[...], preferred_element_type=F32).astype(BF16)
    vt = lax.dot_general(wvt_ref[...], h, (((1,), (1,)), ((), ())), preferred_element_type=F32)
    for j in range(vt_ref.shape[1]):
        vt_ref[0, j] = vt[:, j * LANES:(j + 1) * LANES].astype(BF16)


def _qkv(x, mod, g, wq, wk, wvt, tm):
    b, t, d = x.shape
    const = lambda i, j: (0, 0)
    return pl.pallas_call(
        _qkv_kernel,
        grid=(b, t // tm),
        in_specs=[
            pl.BlockSpec((1, tm, d), lambda i, j: (i, j, 0)),
            pl.BlockSpec((1, 1, 3 * d), lambda i, j: (i, 0, 0)),
            pl.BlockSpec((1, d), const),
            pl.BlockSpec((d, d), const),
            pl.BlockSpec((d, d), const),
            pl.BlockSpec((d, d), const),
        ],
        out_specs=[
            pl.BlockSpec((1, tm, d), lambda i, j: (i, j, 0)),
            pl.BlockSpec((1, tm, d), lambda i, j: (i, j, 0)),
            pl.BlockSpec((1, tm // LANES, d, LANES), lambda i, j: (i, j, 0, 0)),
        ],
        out_shape=[
            jax.ShapeDtypeStruct((b, t, d), BF16),
            jax.ShapeDtypeStruct((b, t, d), BF16),
            jax.ShapeDtypeStruct((b, t // LANES, d, LANES), BF16),
        ],
        compiler_params=_params("parallel", "parallel"),
        name="qkv",
    )(x, mod, g, wq, wk, wvt)


def _na_bias_table(rpb):
    h = rpb.shape[0]
    c = jnp.arange(GRID_W, dtype=jnp.int32)
    cs = jnp.clip(c - NA_KW // 2, 0, GRID_W - NA_KW)
    kc = c[:, None]
    valid = (kc >= cs[None, :]) & (kc < cs[None, :] + NA_KW)
    dc = jnp.clip(kc - c[None, :] + (NA_KW - 1), 0, 2 * NA_KW - 2)
    bt = jnp.where(valid, rpb.astype(F32)[:, :, dc], MASK_BIAS)
    bt = jnp.concatenate([bt, jnp.full((h, 1, GRID_W, GRID_W), MASK_BIAS, F32)], axis=1)
    bt = bt.reshape(h // 2, 2, BIAS_TILES, GRID_W, GRID_W).transpose(0, 2, 3, 1, 4)
    return bt.reshape(h // 2, BIAS_TILES, GRID_W, 2 * GRID_W) * LOG2_E


def _na_kernel(q_ref, k_ref, vt_ref, bias_ref, o_ref, *, rows, pairs_per_step, row_pairs_per_step):
    step = pl.program_id(2)
    lane = lax.broadcasted_iota(jnp.int32, (ROW_PAIR, LANES), 1)
    first_head = lane < HEAD_DIM
    first_head_row = lax.broadcasted_iota(jnp.int32, (GRID_W, LANES), 1) < HEAD_DIM

    def row_pair(j, carry):
        rp = step * row_pairs_per_step + j
        r0 = 2 * rp
        wblk = jnp.clip(rp - NA_KR // 4, 0, (rows - WIN_ROWS) // 2)
        ws = 2 * wblk
        tok = j * ROW_PAIR
        ktok = pl.multiple_of(ws * GRID_W, ROW_PAIR)
        for p in range(pairs_per_step):
            cols = slice(p * LANES, (p + 1) * LANES)
            q2 = q_ref[0, pl.ds(tok, ROW_PAIR), cols]
            qa = jnp.where(first_head, q2, jnp.zeros_like(q2))
            qb = jnp.where(first_head, jnp.zeros_like(q2), q2)
            qbd = jnp.concatenate([qa[:GRID_W], qb[:GRID_W], qa[GRID_W:], qb[GRID_W:]], axis=0)
            kwin = k_ref[0, pl.ds(ktok, WIN_ROWS * GRID_W), cols]
            st = lax.dot_general(kwin, qbd, (((1,), (1,)), ((), ())), preferred_element_type=F32)
            blocks = []
            for i in range(WIN_ROWS):
                kr = ws + i
                tiles = []
                for s in range(2):
                    r = r0 + s
                    rs = jnp.clip(r - NA_KR // 2, 0, rows - NA_KR)
                    inside = jnp.logical_and(kr >= rs, kr < rs + NA_KR)
                    idx = jnp.where(inside, kr - r + (NA_KR - 1), BIAS_TILES - 1)
                    tiles.append(bias_ref[p, idx])
                blocks.append(st[i * GRID_W:(i + 1) * GRID_W] + jnp.concatenate(tiles, axis=1))
            sc = jnp.concatenate(blocks, axis=0)
            m = jnp.max(sc, axis=0, keepdims=True)
            e = jnp.exp2(sc - m)
            l = jnp.sum(e, axis=0, keepdims=True)
            pt = e.astype(BF16)
            ot = jnp.zeros((LANES, 2 * LANES), F32)
            for c in range(WIN_ROWS * GRID_W // LANES):
                vt = vt_ref[0, wblk + c, cols, :]
                ot = ot + jnp.dot(vt, pt[c * LANES:(c + 1) * LANES], preferred_element_type=F32)
            ot = ot * (1.0 / l)
            outs = []
            for s in range(2):
                tt = ot[:, s * LANES:(s + 1) * LANES].T
                outs.append(jnp.where(first_head_row, tt[:GRID_W], tt[GRID_W:]))
            o_ref[0, pl.ds(tok, ROW_PAIR), cols] = jnp.concatenate(outs, axis=0).astype(BF16)
        return carry

    for j in range(row_pairs_per_step):
        row_pair(j, 0)


def _na(q, k, vt, bias, *, head_groups, row_pairs_per_step):
    b, t, d = q.shape
    rows = t // GRID_W
    assert rows % 2 == 0 and rows >= WIN_ROWS and (rows // 2) % row_pairs_per_step == 0
    gc = d // head_groups
    pairs = gc // LANES
    tq = row_pairs_per_step * ROW_PAIR
    kern = functools.partial(_na_kernel, rows=rows, pairs_per_step=pairs, row_pairs_per_step=row_pairs_per_step)
    return pl.pallas_call(
        kern,
        grid=(b, head_groups, t // tq),
        in_specs=[
            pl.BlockSpec((1, tq, gc), lambda i, g, j: (i, j, g)),
            pl.BlockSpec((1, t, gc), lambda i, g, j: (i, 0, g)),
            pl.BlockSpec((1, t // LANES, gc, LANES), lambda i, g, j: (i, 0, g, 0)),
            pl.BlockSpec((pairs, BIAS_TILES, GRID_W, LANES), lambda i, g, j: (g, 0, 0, 0)),
        ],
        out_specs=pl.BlockSpec((1, tq, gc), lambda i, g, j: (i, j, g)),
        out_shape=jax.ShapeDtypeStruct((b, t, d), BF16),
        compiler_params=_params("parallel", "parallel", "parallel"),
        name="na_attn",
    )(q, k, vt, bias)


def _proj_ffn_kernel(x_ref, o_ref, mm_ref, mf_ref, g_ref, wo_ref, wg_ref, wu_ref, wd_ref, out_ref, *, tf):
    d = x_ref.shape[-1]
    m = jnp.dot(o_ref[0], wo_ref[...], preferred_element_type=F32)
    x1 = x_ref[0] + mm_ref[0, :, 2 * d:3 * d] * m
    h = _mod_rmsnorm(x1, g_ref[...], mf_ref[0, :, 0:d], mf_ref[0, :, d:2 * d]).astype(BF16)
    acc = jnp.zeros_like(x1)
    for c in range(wd_ref.shape[0] // tf):
        blk = slice(c * tf, (c + 1) * tf)
        acc = acc + _swiglu_step(h, wg_ref[:, blk], wu_ref[:, blk], wd_ref[blk, :])
    out_ref[0] = x1 + mf_ref[0, :, 2 * d:3 * d] * acc


def _proj_ffn(x, o, mod_mix, mod_ffn, g, wo, wg, wu, wd, tm, tf):
    b, t, d = x.shape
    ff = wd.shape[0]
    tok = lambda i, j: (i, j, 0)
    seq = lambda i, j: (i, 0, 0)
    const = lambda i, j: (0, 0)
    return pl.pallas_call(
        functools.partial(_proj_ffn_kernel, tf=tf),
        grid=(b, t // tm),
        in_specs=[
            pl.BlockSpec((1, tm, d), tok),
            pl.BlockSpec((1, tm, d), tok),
            pl.BlockSpec((1, 1, 3 * d), seq),
            pl.BlockSpec((1, 1, 3 * d), seq),
            pl.BlockSpec((1, d), const),
            pl.BlockSpec((d, d), const),
            pl.BlockSpec((d, ff), const),
            pl.BlockSpec((d, ff), const),
            pl.BlockSpec((ff, d), const),
        ],
        out_specs=pl.BlockSpec((1, tm, d), tok),
        out_shape=jax.ShapeDtypeStruct((b, t, d), F32),
        compiler_params=_params("parallel", "parallel"),
        name="proj_ffn",
    )(x, o, mod_mix, mod_ffn, g, wo, wg, wu, wd)


def _pool_route_kernel(x_ref, xp_ref, xn_ref, mod_ref, g_ref, win_ref, wgrp_ref, ps_ref, wout_ref,
                       modf_ref, gf_ref, wr_ref, br_ref, out_ref, h_ref, w_ref, lp_ref, seg_ref, *, seq_len):
    d = x_ref.shape[-1]
    tm = x_ref.shape[1]
    halo = SUBLANES
    n = tm + 2 * halo
    j = pl.program_id(1)
    x = x_ref[0]
    xe = jnp.concatenate([xp_ref[0], x, xn_ref[0]], axis=0)
    h = _mod_rmsnorm(xe, g_ref[...], mod_ref[0, :, 0:d], mod_ref[0, :, d:2 * d]).astype(BF16)
    u = jnp.dot(h, win_ref[...], preferred_element_type=F32)
    pos = j * tm - halo + lax.broadcasted_iota(jnp.int32, (n, 1), 0)
    u = jnp.where(jnp.logical_and(pos >= 0, pos < seq_len), u, 0.0)
    gc = d // len(POOL_HALF)
    ys = []
    for gi, half in enumerate(POOL_HALF):
        ug = u[:, gi * gc:(gi + 1) * gc]
        fwd = ug
        span = 1
        while span < half:
            fwd = fwd + pltpu.roll(fwd, n - span, axis=0)
            span *= 2
        win = fwd + pltpu.roll(fwd, half, axis=0)
        cnt = jnp.clip(pos + half, 0, seq_len) - jnp.clip(pos - half, 0, seq_len)
        cnt = jnp.maximum(cnt, 1).astype(F32)
        pooled = (win / cnt - ug)[halo:halo + tm].astype(BF16)
        y = jnp.dot(pooled, wgrp_ref[gi], preferred_element_type=F32) * ps_ref[:, gi * gc:(gi + 1) * gc]
        ys.append(y.astype(BF16))
    m = jnp.dot(jnp.concatenate(ys, axis=1), wout_ref[...], preferred_element_type=F32)
    xo = x + mod_ref[0, :, 2 * d:3 * d] * m
    out_ref[0] = xo
    _route(xo, modf_ref, gf_ref, wr_ref, br_ref, h_ref, w_ref, lp_ref, seg_ref)


def _pool_route(x, mod, g, w_in, w_grp, pscale, w_out, mod_ffn, g_ffn, wr, br, tm):
    b, t, d = x.shape
    nt = t // tm
    hb = tm // SUBLANES
    last = t // SUBLANES - 1
    const2 = lambda i, j: (0, 0)
    tok = lambda i, j: (i, j, 0)
    return pl.pallas_call(
        functools.partial(_pool_route_kernel, seq_len=t),
        grid=(b, nt),
        in_specs=[
            pl.BlockSpec((1, tm, d), lambda i, j: (i, j, 0)),
            pl.BlockSpec((1, SUBLANES, d), lambda i, j: (i, jnp.maximum(j * hb - 1, 0), 0)),
            pl.BlockSpec((1, SUBLANES, d), lambda i, j: (i, jnp.minimum((j + 1) * hb, last), 0)),
            pl.BlockSpec((1, 1, 3 * d), lambda i, j: (i, 0, 0)),
            pl.BlockSpec((1, d), const2),
            pl.BlockSpec((d, d), const2),
            pl.BlockSpec(w_grp.shape, lambda i, j: (0, 0, 0)),
            pl.BlockSpec((1, d), const2),
            pl.BlockSpec((d, d), const2),
            pl.BlockSpec((1, 1, 3 * d), lambda i, j: (i, 0, 0)),
            pl.BlockSpec((1, d), const2),
            pl.BlockSpec((d, LANES), const2),
            pl.BlockSpec((1, LANES), const2),
        ],
        out_specs=[
            pl.BlockSpec((1, tm, d), tok),
            pl.BlockSpec((1, tm, d), tok),
            pl.BlockSpec((1, tm, LANES), tok),
            pl.BlockSpec((1, tm, LANES), tok),
            pl.BlockSpec((1, SUBLANES, LANES), lambda i, j: (i * nt + j, 0, 0)),
        ],
        out_shape=[
            jax.ShapeDtypeStruct((b, t, d), F32),
            jax.ShapeDtypeStruct((b, t, d), BF16),
            jax.ShapeDtypeStruct((b, t, LANES), F32),
            jax.ShapeDtypeStruct((b, t, LANES), F32),
            jax.ShapeDtypeStruct((b * nt, SUBLANES, LANES), jnp.int32),
        ],
        compiler_params=_params("parallel", "parallel"),
        name="pool_route",
    )(x, x, x, mod, g, w_in, w_grp, pscale, w_out, mod_ffn, g_ffn, wr, br)


def _group_rows(tm):
    return TOP_K * tm + N_EXPERTS * BF16_ROWS


def _route(x, mod_ref, g_ref, wr_ref, br_ref, h_ref, w_ref, lp_ref, seg_ref):
    tm, d = x.shape
    h = _mod_rmsnorm(x, g_ref[...], mod_ref[0, :, 0:d], mod_ref[0, :, d:2 * d]).astype(BF16)
    h_ref[0] = h
    logits = jnp.dot(h, wr_ref[...], preferred_element_type=F32) + br_ref[...]
    lane = lax.broadcasted_iota(jnp.int32, logits.shape, 1).astype(F32)
    v1 = jnp.max(logits, axis=-1, keepdims=True)
    i1 = jnp.min(jnp.where(logits == v1, lane, float(LANES)), axis=-1, keepdims=True)
    rest = jnp.where(lane == i1, -jnp.inf, logits)
    v2 = jnp.max(rest, axis=-1, keepdims=True)
    i2 = jnp.min(jnp.where(rest == v2, lane, float(LANES)), axis=-1, keepdims=True)
    e2 = jnp.exp(v2 - v1)
    w1 = 1.0 / (1.0 + e2)
    w2 = e2 / (1.0 + e2)
    sel = jnp.where(lane == i1, 1.0, jnp.where(lane == i2, 1.0, 0.0))
    row = lax.broadcasted_iota(jnp.int32, (tm, tm), 0)
    col = lax.broadcasted_iota(jnp.int32, (tm, tm), 1)
    before = jnp.where(col < row, 1.0, 0.0).astype(BF16)
    rank = jnp.dot(before, sel.astype(BF16), preferred_element_type=F32)
    count = jnp.sum(sel, axis=0, keepdims=True)
    tiles = jnp.floor((count + (BF16_ROWS - 1)) * (1.0 / BF16_ROWS))
    er = lax.broadcasted_iota(jnp.int32, (LANES, LANES), 0)
    ec = lax.broadcasted_iota(jnp.int32, (LANES, LANES), 1)
    lower = jnp.where(er < ec, 1.0, 0.0).astype(BF16)
    tiles8 = jnp.broadcast_to(tiles, (SUBLANES, LANES)).astype(BF16)
    start = jnp.dot(tiles8, lower, preferred_element_type=F32)[0:1] * BF16_ROWS
    slot = start + rank
    lp1 = jnp.sum(jnp.where(lane == i1, slot, 0.0), axis=-1, keepdims=True)
    lp2 = jnp.sum(jnp.where(lane == i2, slot, 0.0), axis=-1, keepdims=True)
    w_ref[0] = jnp.where(lane == 0, w1, jnp.where(lane == 1, w2, 0.0))
    lp_ref[0] = jnp.where(lane == 0, lp1, jnp.where(lane == 1, lp2, 0.0))
    sub = lax.broadcasted_iota(jnp.int32, (SUBLANES, LANES), 0)
    seg = jnp.where(sub == 0, tiles * BF16_ROWS, jnp.where(sub == 1, start, 0.0))
    seg_ref[0] = seg.astype(jnp.int32)


def _segment_copies(tile, dst_ref, n_ref, ls_ref, make_copy, act):
    for e in range(N_EXPERTS):
        n = n_ref[tile * N_EXPERTS + e]
        ls = ls_ref[tile * N_EXPERTS + e]
        dst = dst_ref[tile * N_EXPERTS + e]
        size = BF16_ROWS
        while size <= TOKEN_TILE:
            done = n & (-2 * size)

            @pl.when((n & size) != 0)
            def _(size=size, done=done, ls=ls, dst=dst):
                act(make_copy(pl.multiple_of(ls + done, BF16_ROWS), pl.multiple_of(dst + done, BF16_ROWS), size))

            size *= 2


def _dispatch_kernel(dst_ref, n_ref, ls_ref, h_ref, lp_ref, xs_ref, grp_s, zero_s, sem, seg_sem):
    tile = pl.program_id(0)
    n_tiles = pl.num_programs(0)
    tm = h_ref.shape[0]

    @pl.when(tile == 0)
    def _():
        zero_s[...] = jnp.zeros_like(zero_s)

        def make_fill(local, glob, size):
            del local
            return pltpu.make_async_copy(zero_s.at[pl.ds(0, size), :], xs_ref.at[pl.ds(glob, size), :], sem)

        _segment_copies(n_tiles, dst_ref, n_ref, ls_ref, make_fill, lambda c: c.start())
        _segment_copies(n_tiles, dst_ref, n_ref, ls_ref, make_fill, lambda c: c.wait())

        first = dst_ref[(n_tiles + 1) * N_EXPERTS]
        count = n_ref[(n_tiles + 1) * N_EXPERTS]

        def fill_tile(k):
            return make_fill(0, pl.multiple_of(first + k * tm, tm), tm)

        lax.fori_loop(0, count, lambda k, c: (fill_tile(k).start(), c)[1], 0)
        lax.fori_loop(0, count, lambda k, c: (fill_tile(k).wait(), c)[1], 0)

    gb = grp_s.shape[1]
    buf = tile % 2
    lpt = lp_ref[...].T
    slot = lax.broadcasted_iota(jnp.int32, (gb, tm), 0).astype(F32)
    onehot = jnp.where(slot == lpt[0:1, :], 1.0, jnp.where(slot == lpt[1:2, :], 1.0, 0.0)).astype(BF16)
    grp_s[buf] = jnp.dot(onehot, h_ref[...], preferred_element_type=F32).astype(BF16)

    def copies(t, b, act):
        def make_copy(local, glob, size):
            return pltpu.make_async_copy(grp_s.at[b, pl.ds(local, size), :], xs_ref.at[pl.ds(glob, size), :], seg_sem.at[b])

        _segment_copies(t, dst_ref, n_ref, ls_ref, make_copy, act)

    copies(tile, buf, lambda c: c.start())

    @pl.when(tile > 0)
    def _():
        copies(tile - 1, 1 - buf, lambda c: c.wait())

    @pl.when(tile == n_tiles - 1)
    def _():
        copies(tile, buf, lambda c: c.wait())


def _dispatch(h, lp, dst, seg_n, seg_ls, slots, tm):
    n, d = h.shape
    assert tm == TOKEN_TILE
    grid_spec = pltpu.PrefetchScalarGridSpec(
        num_scalar_prefetch=3,
        grid=(n // tm,),
        in_specs=[
            pl.BlockSpec((tm, d), lambda i, *_: (i, 0)),
            pl.BlockSpec((tm, LANES), lambda i, *_: (i, 0)),
        ],
        out_specs=pl.BlockSpec(memory_space=pl.ANY),
        scratch_shapes=[pltpu.VMEM((2, _group_rows(tm), d), BF16), pltpu.VMEM((tm, d), BF16),
                        pltpu.SemaphoreType.DMA(()), pltpu.SemaphoreType.DMA((2,))],
    )
    return pl.pallas_call(
        _dispatch_kernel,
        grid_spec=grid_spec,
        out_shape=jax.ShapeDtypeStruct((slots, d), BF16),
        compiler_params=_params("arbitrary"),
        name="moe_dispatch",
    )(dst, seg_n, seg_ls, h, lp)


def _swiglu_step(h, wg, wu, wd):
    tf = wd.shape[0]
    au = jnp.dot(h, jnp.concatenate([wg, wu], axis=1), preferred_element_type=F32)
    return jnp.dot((_silu(au[:, :tf]) * au[:, tf:]).astype(BF16), wd, preferred_element_type=F32)


def _experts_kernel(te_ref, tr_ref, ts_ref, xs_ref, wg_ref, wu_ref, wd_ref, ys_ref, acc_s):
    del te_ref, ts_ref
    i = pl.program_id(0)
    f = pl.program_id(1)
    rows = tr_ref[i]
    part = xs_ref.shape[0] // EXPERT_TILE_PARTS

    @pl.when(f == 0)
    def _():
        acc_s[...] = jnp.zeros_like(acc_s)

    for k in range(1, EXPERT_TILE_PARTS + 1):
        @pl.when(jnp.logical_and(rows > (k - 1) * part, rows <= k * part))
        def _(used=k * part):
            acc_s[0:used, :] += _swiglu_step(xs_ref[0:used, :], wg_ref[0], wu_ref[0], wd_ref[0])

    @pl.when(f == pl.num_programs(1) - 1)
    def _():
        ys_ref[...] = acc_s[...].astype(BF16)


def _experts(xs, tile_expert, tile_rows, tile_src, wg, wu, wd, tm, tf):
    slots, d = xs.shape
    nf = wd.shape[1] // tf
    fidx = lambda f, tr, i: jnp.where(tr[i] > 0, f, nf - 1)
    grid_spec = pltpu.PrefetchScalarGridSpec(
        num_scalar_prefetch=3,
        grid=(slots // tm, nf),
        in_specs=[
            pl.BlockSpec((tm, d), lambda i, f, te, tv, ts: (ts[i], 0)),
            pl.BlockSpec((1, d, tf), lambda i, f, te, tv, ts: (te[i], 0, fidx(f, tv, i))),
            pl.BlockSpec((1, d, tf), lambda i, f, te, tv, ts: (te[i], 0, fidx(f, tv, i))),
            pl.BlockSpec((1, tf, d), lambda i, f, te, tv, ts: (te[i], fidx(f, tv, i), 0)),
        ],
        out_specs=pl.BlockSpec((tm, d), lambda i, f, te, tv, ts: (i, 0)),
        scratch_shapes=[pltpu.VMEM((tm, d), F32)],
    )
    return pl.pallas_call(
        _experts_kernel,
        grid_spec=grid_spec,
        out_shape=jax.ShapeDtypeStruct((slots, d), BF16),
        compiler_params=_params("parallel", "arbitrary"),
        name="moe_experts",
    )(tile_expert, tile_rows, tile_src, xs, wg, wu, wd)


def _combine_kernel(dst_ref, n_ref, ls_ref, x_ref, w_ref, lp_ref, mod_ref, gf_ref, ys_ref, out_ref, y_s, sem):
    d = x_ref.shape[-1]
    tm = x_ref.shape[1]
    gb = y_s.shape[1]
    tile = pl.program_id(0) * pl.num_programs(1) + pl.program_id(1)
    n_tiles = pl.num_programs(0) * pl.num_programs(1)
    buf = tile % 2

    def fetch(t, b, act):
        def make_copy(local, glob, size):
            return pltpu.make_async_copy(ys_ref.at[pl.ds(glob, size), :], y_s.at[b, pl.ds(local, size), :], sem.at[b])

        _segment_copies(t, dst_ref, n_ref, ls_ref, make_copy, act)

    def start_fetch(t, b):
        y_s[b, TOP_K * tm:, :] = jnp.zeros((gb - TOP_K * tm, d), BF16)
        fetch(t, b, lambda c: c.start())

    @pl.when(tile == 0)
    def _():
        start_fetch(tile, buf)

    @pl.when(tile + 1 < n_tiles)
    def _():
        start_fetch(tile + 1, 1 - buf)

    fetch(tile, buf, lambda c: c.wait())

    y = y_s[buf]
    lp = lp_ref[0]
    w = w_ref[0]
    slot = lax.broadcasted_iota(jnp.int32, (tm, gb), 1).astype(F32)
    y1 = jnp.dot(jnp.where(slot == lp[:, 0:1], 1.0, 0.0).astype(BF16), y, preferred_element_type=F32)
    y2 = jnp.dot(jnp.where(slot == lp[:, 1:2], 1.0, 0.0).astype(BF16), y, preferred_element_type=F32)
    f = w[:, 0:1] * y1 + w[:, 1:2] * y2
    x = x_ref[0] + mod_ref[0, :, 2 * d:3 * d] * f
    var = jnp.mean(x * x, axis=-1, keepdims=True)
    out_ref[0] = (x * lax.rsqrt(var + RMS_EPS)) * gf_ref[...]


def _combine(x, w, lp, mod, gf, ys, dst, seg_n, seg_ls, tm):
    b, t, d = x.shape
    assert tm == TOKEN_TILE
    tok = lambda i, j, *_: (i, j, 0)
    grid_spec = pltpu.PrefetchScalarGridSpec(
        num_scalar_prefetch=3,
        grid=(b, t // tm),
        in_specs=[
            pl.BlockSpec((1, tm, d), tok),
            pl.BlockSpec((1, tm, LANES), tok),
            pl.BlockSpec((1, tm, LANES), tok),
            pl.BlockSpec((1, 1, 3 * d), lambda i, j, *_: (i, 0, 0)),
            pl.BlockSpec((1, d), lambda i, j, *_: (0, 0)),
            pl.BlockSpec(memory_space=pl.ANY),
        ],
        out_specs=pl.BlockSpec((1, tm, d), tok),
        scratch_shapes=[pltpu.VMEM((2, _group_rows(tm), d), BF16), pltpu.SemaphoreType.DMA((2,))],
    )
    return pl.pallas_call(
        _combine_kernel,
        grid_spec=grid_spec,
        out_shape=jax.ShapeDtypeStruct((b, t, d), F32),
        compiler_params=_params("arbitrary", "arbitrary"),
        name="moe_combine",
    )(dst, seg_n, seg_ls, x, w, lp, mod, gf, ys)


def _trunk(x, mods_mix, mods_ffn, p):
    b, t, d = x.shape
    tm = TOKEN_TILE
    assert t % tm == 0

    q, k, vt = _qkv(x, mods_mix[0], p["ln_mix_g"][0], p["wq"], p["wk"], p["wvt"], _largest_tile(t, QKV_TILE))
    rows = t // GRID_W
    rpp = 16 if (rows // 2) % 16 == 0 else 1
    o = _na(q, k, vt, p["na_bias"], head_groups=2, row_pairs_per_step=rpp)
    ff = p["ffn_wd"].shape[0]
    x = _proj_ffn(x, o, mods_mix[0], mods_ffn[0], p["ln_ffn_g"][0], p["wo"], p["ffn_wg"], p["ffn_wu"], p["ffn_wd"],
                  tm, _largest_tile(ff, FF_TILE_DENSE))

    x, h, w, lp, seg = _pool_route(x, mods_mix[1], p["ln_mix_g"][1], p["pool_w_in"], p["pool_w_grp"], p["pool_scale"],
                                   p["pool_w_out"], mods_ffn[1], p["ln_ffn_g"][1], p["wr"], p["br"], tm)

    n = b * t
    nt = n // tm
    seg_n = seg[:, 0, :N_EXPERTS]
    seg_ls = seg[:, 1, :N_EXPERTS]
    before = jnp.cumsum(seg_n, axis=0) - seg_n
    total = jnp.sum(seg_n, axis=0)
    te = EXPERT_TILE
    group = ((total + te - 1) // te) * te
    ends = jnp.cumsum(group)
    dst = (ends - group)[None, :] + before
    n_tiles = -(-(TOP_K * n + nt * N_EXPERTS * (BF16_ROWS - 1)) // te) + N_EXPERTS
    tile_start = jnp.arange(n_tiles, dtype=jnp.int32) * te
    tile_live = (tile_start < ends[-1]).astype(jnp.int32)
    tile_expert = jnp.sum((ends[None, :] <= tile_start[:, None]).astype(jnp.int32), axis=1)
    last_expert = jnp.sum((ends <= ends[-1] - 1).astype(jnp.int32))
    tile_expert = jnp.minimum(jnp.where(tile_live > 0, tile_expert, last_expert), N_EXPERTS - 1)
    tile_src = jnp.minimum(jnp.arange(n_tiles, dtype=jnp.int32), jnp.sum(tile_live) - 1)
    filled_end = (ends - group + total)[tile_expert]
    tile_rows = jnp.where(tile_live > 0, jnp.clip(filled_end - tile_start, 0, te), 0).astype(jnp.int32)
    lead = jnp.arange(N_EXPERTS) == 0
    fills = (n_tiles * te - ends[-1]) // tm
    dst = jnp.concatenate([dst, (ends - group + total)[None, :], jnp.where(lead, ends[-1], 0)[None, :]], axis=0)
    seg_n = jnp.concatenate([seg_n, (group - total)[None, :], jnp.where(lead, fills, 0)[None, :]], axis=0)
    seg_ls = jnp.concatenate([seg_ls, jnp.zeros((2, N_EXPERTS), seg_ls.dtype)], axis=0)
    dst, seg_n, seg_ls = (a.reshape(-1).astype(jnp.int32) for a in (dst, seg_n, seg_ls))

    xs = _dispatch(h.reshape(n, d), lp.reshape(n, LANES), dst, seg_n, seg_ls, n_tiles * te, tm)
    ys = _experts(xs, tile_expert, tile_rows, tile_src, p["moe_wg"], p["moe_wu"], p["moe_wd"], te,
                  _largest_tile(ff, FF_TILE_MOE))
    return _combine(x, w, lp, mods_ffn[1], p["ln_f_g"], ys, dst, seg_n, seg_ls, tm)


def kernel(x_prompt, x_sample, c_prompt, c_sample, ln_mix_g, ada_mix_w, ada_mix_b, ln_ffn_g, ada_ffn_w, ada_ffn_b, na_w_qkv, na_rpb, na_w_o, pool_w_in, pool_w_grp, pool_scale, pool_w_out, ffn_w_gate, ffn_w_up, ffn_w_down, moe_w_router, moe_b_router, moe_w_gate, moe_w_up, moe_w_down, ln_f_g):
    d = x_prompt.shape[-1]
    depth = ln_mix_g.shape[0]
    assert depth == 2 and d == NA_HEADS * HEAD_DIM
    bp, bs = c_prompt.shape[0], c_sample.shape[0]

    pad = (-(bp + bs)) % SUBLANES
    c_all = jnp.concatenate([c_prompt, c_sample, jnp.zeros((pad, d), F32)], axis=0)
    mods_mix = _adaln(c_all, ada_mix_w, ada_mix_b)
    mods_ffn = _adaln(c_all, ada_ffn_w, ada_ffn_b)

    wqkv = na_w_qkv[0]
    ne = moe_w_router.shape[-1]
    assert ne == N_EXPERTS
    p = {
        "ln_mix_g": ln_mix_g.reshape(depth, 1, d),
        "ln_ffn_g": ln_ffn_g.reshape(depth, 1, d),
        "ln_f_g": ln_f_g.reshape(1, d),
        "wq": wqkv[:, 0:d].astype(BF16),
        "wk": wqkv[:, d:2 * d].astype(BF16),
        "wvt": wqkv[:, 2 * d:3 * d].T.astype(BF16),
        "na_bias": _na_bias_table(na_rpb[0]),
        "wo": na_w_o[0].astype(BF16),
        "ffn_wg": ffn_w_gate[0].astype(BF16),
        "ffn_wu": ffn_w_up[0].astype(BF16),
        "ffn_wd": ffn_w_down[0].astype(BF16),
        "pool_w_in": pool_w_in[0].astype(BF16),
        "pool_w_grp": pool_w_grp[0].astype(BF16),
        "pool_scale": pool_scale[0].reshape(1, d),
        "pool_w_out": pool_w_out[0].astype(BF16),
        "wr": jnp.pad(moe_w_router[0], ((0, 0), (0, LANES - ne))).astype(BF16),
        "br": jnp.pad(moe_b_router[0].astype(F32), (0, LANES - ne), constant_values=MASK_BIAS).reshape(1, LANES),
        "moe_wg": moe_w_gate[0].astype(BF16),
        "moe_wu": moe_w_up[0].astype(BF16),
        "moe_wd": moe_w_down[0].astype(BF16),
    }

    def group_mods(m, lo, n):
        return m[:, lo:lo + n, None, :]

    y_prompt = _trunk(x_prompt, group_mods(mods_mix, 0, bp), group_mods(mods_ffn, 0, bp), p)
    y_sample = _trunk(x_sample, group_mods(mods_mix, bp, bs), group_mods(mods_ffn, bp, bs), p)
    return (y_prompt, y_sample)
```

```python
import functools

import jax
import jax.numpy as jnp
from jax import lax
from jax.experimental import pallas as pl
from jax.experimental.pallas import tpu as pltpu

F32 = jnp.float32
BF16 = jnp.bfloat16

RMS_EPS = 1e-6
GRID_W = 64
NA_HEADS = 16
HEAD_DIM = 64
NA_KR = 8
NA_KW = 16
POOL_HALF = (1, 2, 4, 8)
N_EXPERTS = 8
TOP_K = 2
LANES = 128
SUBLANES = 8
BF16_ROWS = 16
MASK_BIAS = -1e30
LOG2_E = 1.4426950408889634
ROW_PAIR = 2 * GRID_W
WIN_ROWS = NA_KR + 2
BIAS_TILES = 2 * NA_KR
VMEM_LIMIT = 56 * 1024 * 1024
TOKEN_TILE = 512
QKV_TILE = 1024
FF_TILE_DENSE = 1792
FF_TILE_MOE = 1792
EXPERT_TILE = 512
EXPERT_TILE_PARTS = 4


def _silu(a):
    return a * jax.nn.sigmoid(a)


def _mod_rmsnorm(x, g, shift, scale):
    var = jnp.mean(x * x, axis=-1, keepdims=True)
    y = (x * lax.rsqrt(var + RMS_EPS)) * g
    return y * (1.0 + scale) + shift


def _params(*sem, vmem=VMEM_LIMIT):
    return pltpu.CompilerParams(dimension_semantics=sem, vmem_limit_bytes=vmem)


def _largest_tile(total, target):
    best = LANES
    for cand in range(LANES, min(total, target) + 1, LANES):
        if total % cand == 0:
            best = cand
    assert total % best == 0
    return best


def _adaln_kernel(c_ref, w_ref, b_ref, o_ref):
    s = _silu(c_ref[...]).astype(BF16)
    o_ref[0] = jnp.dot(s, w_ref[0].astype(BF16), preferred_element_type=F32) + b_ref[0]


def _adaln(c_all, w, b):
    depth, d, d3 = w.shape
    r = c_all.shape[0]
    tn = _largest_tile(d3, 1024)
    return pl.pallas_call(
        _adaln_kernel,
        grid=(depth, d3 // tn),
        in_specs=[
            pl.BlockSpec((r, d), lambda i, j: (0, 0)),
            pl.BlockSpec((1, d, tn), lambda i, j: (i, 0, j)),
            pl.BlockSpec((1, 1, tn), lambda i, j: (i, 0, j)),
        ],
        out_specs=pl.BlockSpec((1, r, tn), lambda i, j: (i, 0, j)),
        out_shape=jax.ShapeDtypeStruct((depth, r, d3), F32),
        compiler_params=_params("parallel", "parallel"),
        name="adaln",
    )(c_all, w, b.reshape(depth, 1, d3))


def _qkv_kernel(x_ref, mod_ref, g_ref, wq_ref, wk_ref, wvt_ref, q_ref, k_ref, vt_ref):
    d = x_ref.shape[-1]
    h = _mod_rmsnorm(x_ref[0], g_ref[...], mod_ref[0, :, 0:d], mod_ref[0, :, d:2 * d]).astype(BF16)
    q = jnp.dot(h, wq_ref[...], preferred_element_type=F32) * (HEAD_DIM ** -0.5 * LOG2_E)
    q_ref[0] = q.astype(BF16)
    k_ref[0] = jnp.dot(h, wk_ref[...], preferred_element_type=F32).astype(BF16)
    vt = lax.dot_general(wvt_ref[...], h, (((1,), (1,)), ((), ())), preferred_element_type=F32)
    for j in range(vt_ref.shape[1]):
        vt_ref[0, j] = vt[:, j * LANES:(j + 1) * LANES].astype(BF16)


def _qkv(x, mod, g, wq, wk, wvt, tm):
    b, t, d = x.shape
    const = lambda i, j: (0, 0)
    return pl.pallas_call(
        _qkv_kernel,
        grid=(b, t // tm),
        in_specs=[
            pl.BlockSpec((1, tm, d), lambda i, j: (i, j, 0)),
            pl.BlockSpec((1, 1, 3 * d), lambda i, j: (i, 0, 0)),
            pl.BlockSpec((1, d), const),
            pl.BlockSpec((d, d), const),
            pl.BlockSpec((d, d), const),
            pl.BlockSpec((d, d), const),
        ],
        out_specs=[
            pl.BlockSpec((1, tm, d), lambda i, j: (i, j, 0)),
            pl.BlockSpec((1, tm, d), lambda i, j: (i, j, 0)),
            pl.BlockSpec((1, tm // LANES, d, LANES), lambda i, j: (i, j, 0, 0)),
        ],
        out_shape=[
            jax.ShapeDtypeStruct((b, t, d), BF16),
            jax.ShapeDtypeStruct((b, t, d), BF16),
            jax.ShapeDtypeStruct((b, t // LANES, d, LANES), BF16),
        ],
        compiler_params=_params("parallel", "parallel"),
        name="qkv",
    )(x, mod, g, wq, wk, wvt)


def _na_bias_table(rpb):
    h = rpb.shape[0]
    c = jnp.arange(GRID_W, dtype=jnp.int32)
    cs = jnp.clip(c - NA_KW // 2, 0, GRID_W - NA_KW)
    kc = c[:, None]
    valid = (kc >= cs[None, :]) & (kc < cs[None, :] + NA_KW)
    dc = jnp.clip(kc - c[None, :] + (NA_KW - 1), 0, 2 * NA_KW - 2)
    bt = jnp.where(valid, rpb.astype(F32)[:, :, dc], MASK_BIAS)
    bt = jnp.concatenate([bt, jnp.full((h, 1, GRID_W, GRID_W), MASK_BIAS, F32)], axis=1)
    bt = bt.reshape(h // 2, 2, BIAS_TILES, GRID_W, GRID_W).transpose(0, 2, 3, 1, 4)
    return bt.reshape(h // 2, BIAS_TILES, GRID_W, 2 * GRID_W) * LOG2_E


def _na_kernel(q_ref, k_ref, vt_ref, bias_ref, o_ref, *, rows, pairs_per_step, row_pairs_per_step):
    step = pl.program_id(2)
    lane = lax.broadcasted_iota(jnp.int32, (ROW_PAIR, LANES), 1)
    first_head = lane < HEAD_DIM
    first_head_row = lax.broadcasted_iota(jnp.int32, (GRID_W, LANES), 1) < HEAD_DIM

    def row_pair(j, carry):
        rp = step * row_pairs_per_step + j
        r0 = 2 * rp
        wblk = jnp.clip(rp - NA_KR // 4, 0, (rows - WIN_ROWS) // 2)
        ws = 2 * wblk
        tok = j * ROW_PAIR
        ktok = pl.multiple_of(ws * GRID_W, ROW_PAIR)
        for p in range(pairs_per_step):
            cols = slice(p * LANES, (p + 1) * LANES)
            q2 = q_ref[0, pl.ds(tok, ROW_PAIR), cols]
            qa = jnp.where(first_head, q2, jnp.zeros_like(q2))
            qb = jnp.where(first_head, jnp.zeros_like(q2), q2)
            qbd = jnp.concatenate([qa[:GRID_W], qb[:GRID_W], qa[GRID_W:], qb[GRID_W:]], axis=0)
            kwin = k_ref[0, pl.ds(ktok, WIN_ROWS * GRID_W), cols]
            st = lax.dot_general(kwin, qbd, (((1,), (1,)), ((), ())), preferred_element_type=F32)
            blocks = []
            for i in range(WIN_ROWS):
                kr = ws + i
                tiles = []
                for s in range(2):
                    r = r0 + s
                    rs = jnp.clip(r - NA_KR // 2, 0, rows - NA_KR)
                    inside = jnp.logical_and(kr >= rs, kr < rs + NA_KR)
                    idx = jnp.where(inside, kr - r + (NA_KR - 1), BIAS_TILES - 1)
                    tiles.append(bias_ref[p, idx])
                blocks.append(st[i * GRID_W:(i + 1) * GRID_W] + jnp.concatenate(tiles, axis=1))
            sc = jnp.concatenate(blocks, axis=0)
            m = jnp.max(sc, axis=0, keepdims=True)
            e = jnp.exp2(sc - m)
            l = jnp.sum(e, axis=0, keepdims=True)
            pt = e.astype(BF16)
            ot = jnp.zeros((LANES, 2 * LANES), F32)
            for c in range(WIN_ROWS * GRID_W // LANES):
                vt = vt_ref[0, wblk + c, cols, :]
                ot = ot + jnp.dot(vt, pt[c * LANES:(c + 1) * LANES], preferred_element_type=F32)
            ot = ot * (1.0 / l)
            outs = []
            for s in range(2):
                tt = ot[:, s * LANES:(s + 1) * LANES].T
                outs.append(jnp.where(first_head_row, tt[:GRID_W], tt[GRID_W:]))
            o_ref[0, pl.ds(tok, ROW_PAIR), cols] = jnp.concatenate(outs, axis=0).astype(BF16)
        return carry

    for j in range(row_pairs_per_step):
        row_pair(j, 0)


def _na(q, k, vt, bias, *, head_groups, row_pairs_per_step):
    b, t, d = q.shape
    rows = t // GRID_W
    assert rows % 2 == 0 and rows >= WIN_ROWS and (rows // 2) % row_pairs_per_step == 0
    gc = d // head_groups
    pairs = gc // LANES
    tq = row_pairs_per_step * ROW_PAIR
    kern = functools.partial(_na_kernel, rows=rows, pairs_per_step=pairs, row_pairs_per_step=row_pairs_per_step)
    return pl.pallas_call(
        kern,
        grid=(b, head_groups, t // tq),
        in_specs=[
            pl.BlockSpec((1, tq, gc), lambda i, g, j: (i, j, g)),
            pl.BlockSpec((1, t, gc), lambda i, g, j: (i, 0, g)),
            pl.BlockSpec((1, t // LANES, gc, LANES), lambda i, g, j: (i, 0, g, 0)),
            pl.BlockSpec((pairs, BIAS_TILES, GRID_W, LANES), lambda i, g, j: (g, 0, 0, 0)),
        ],
        out_specs=pl.BlockSpec((1, tq, gc), lambda i, g, j: (i, j, g)),
        out_shape=jax.ShapeDtypeStruct((b, t, d), BF16),
        compiler_params=_params("parallel", "parallel", "parallel"),
        name="na_attn",
    )(q, k, vt, bias)


def _proj_ffn_kernel(x_ref, o_ref, mm_ref, mf_ref, g_ref, wo_ref, wg_ref, wu_ref, wd_ref, out_ref, *, tf):
    d = x_ref.shape[-1]
    m = jnp.dot(o_ref[0], wo_ref[...], preferred_element_type=F32)
    x1 = x_ref[0] + mm_ref[0, :, 2 * d:3 * d] * m
    h = _mod_rmsnorm(x1, g_ref[...], mf_ref[0, :, 0:d], mf_ref[0, :, d:2 * d]).astype(BF16)
    acc = jnp.zeros_like(x1)
    for c in range(wd_ref.shape[0] // tf):
        blk = slice(c * tf, (c + 1) * tf)
        acc = acc + _swiglu_step(h, wg_ref[:, blk], wu_ref[:, blk], wd_ref[blk, :])
    out_ref[0] = x1 + mf_ref[0, :, 2 * d:3 * d] * acc


def _proj_ffn(x, o, mod_mix, mod_ffn, g, wo, wg, wu, wd, tm, tf):
    b, t, d = x.shape
    ff = wd.shape[0]
    tok = lambda i, j: (i, j, 0)
    seq = lambda i, j: (i, 0, 0)
    const = lambda i, j: (0, 0)
    return pl.pallas_call(
        functools.partial(_proj_ffn_kernel, tf=tf),
        grid=(b, t // tm),
        in_specs=[
            pl.BlockSpec((1, tm, d), tok),
            pl.BlockSpec((1, tm, d), tok),
            pl.BlockSpec((1, 1, 3 * d), seq),
            pl.BlockSpec((1, 1, 3 * d), seq),
            pl.BlockSpec((1, d), const),
            pl.BlockSpec((d, d), const),
            pl.BlockSpec((d, ff), const),
            pl.BlockSpec((d, ff), const),
            pl.BlockSpec((ff, d), const),
        ],
        out_specs=pl.BlockSpec((1, tm, d), tok),
        out_shape=jax.ShapeDtypeStruct((b, t, d), F32),
        compiler_params=_params("parallel", "parallel"),
        name="proj_ffn",
    )(x, o, mod_mix, mod_ffn, g, wo, wg, wu, wd)


def _pool_route_kernel(x_ref, xp_ref, xn_ref, mod_ref, g_ref, win_ref, wgrp_ref, ps_ref, wout_ref,
                       modf_ref, gf_ref, wr_ref, br_ref, out_ref, h_ref, w_ref, lp_ref, seg_ref, *, seq_len):
    d = x_ref.shape[-1]
    tm = x_ref.shape[1]
    halo = SUBLANES
    n = tm + 2 * halo
    j = pl.program_id(1)
    x = x_ref[0]
    xe = jnp.concatenate([xp_ref[0], x, xn_ref[0]], axis=0)
    h = _mod_rmsnorm(xe, g_ref[...], mod_ref[0, :, 0:d], mod_ref[0, :, d:2 * d]).astype(BF16)
    u = jnp.dot(h, win_ref[...], preferred_element_type=F32)
    pos = j * tm - halo + lax.broadcasted_iota(jnp.int32, (n, 1), 0)
    u = jnp.where(jnp.logical_and(pos >= 0, pos < seq_len), u, 0.0)
    gc = d // len(POOL_HALF)
    ys = []
    for gi, half in enumerate(POOL_HALF):
        ug = u[:, gi * gc:(gi + 1) * gc]
        fwd = ug
        span = 1
        while span < half:
            fwd = fwd + pltpu.roll(fwd, n - span, axis=0)
            span *= 2
        win = fwd + pltpu.roll(fwd, half, axis=0)
        cnt = jnp.clip(pos + half, 0, seq_len) - jnp.clip(pos - half, 0, seq_len)
        cnt = jnp.maximum(cnt, 1).astype(F32)
        pooled = (win / cnt - ug)[halo:halo + tm].astype(BF16)
        y = jnp.dot(pooled, wgrp_ref[gi], preferred_element_type=F32) * ps_ref[:, gi * gc:(gi + 1) * gc]
        ys.append(y.astype(BF16))
    m = jnp.dot(jnp.concatenate(ys, axis=1), wout_ref[...], preferred_element_type=F32)
    xo = x + mod_ref[0, :, 2 * d:3 * d] * m
    out_ref[0] = xo
    _route(xo, modf_ref, gf_ref, wr_ref, br_ref, h_ref, w_ref, lp_ref, seg_ref)


def _pool_route(x, mod, g, w_in, w_grp, pscale, w_out, mod_ffn, g_ffn, wr, br, tm):
    b, t, d = x.shape
    nt = t // tm
    hb = tm // SUBLANES
    last = t // SUBLANES - 1
    const2 = lambda i, j: (0, 0)
    tok = lambda i, j: (i, j, 0)
    return pl.pallas_call(
        functools.partial(_pool_route_kernel, seq_len=t),
        grid=(b, nt),
        in_specs=[
            pl.BlockSpec((1, tm, d), lambda i, j: (i, j, 0)),
            pl.BlockSpec((1, SUBLANES, d), lambda i, j: (i, jnp.maximum(j * hb - 1, 0), 0)),
            pl.BlockSpec((1, SUBLANES, d), lambda i, j: (i, jnp.minimum((j + 1) * hb, last), 0)),
            pl.BlockSpec((1, 1, 3 * d), lambda i, j: (i, 0, 0)),
            pl.BlockSpec((1, d), const2),
            pl.BlockSpec((d, d), const2),
            pl.BlockSpec(w_grp.shape, lambda i, j: (0, 0, 0)),
            pl.BlockSpec((1, d), const2),
            pl.BlockSpec((d, d), const2),
            pl.BlockSpec((1, 1, 3 * d), lambda i, j: (i, 0, 0)),
            pl.BlockSpec((1, d), const2),
            pl.BlockSpec((d, LANES), const2),
            pl.BlockSpec((1, LANES), const2),
        ],
        out_specs=[
            pl.BlockSpec((1, tm, d), tok),
            pl.BlockSpec((1, tm, d), tok),
            pl.BlockSpec((1, tm, LANES), tok),
            pl.BlockSpec((1, tm, LANES), tok),
            pl.BlockSpec((1, SUBLANES, LANES), lambda i, j: (i * nt + j, 0, 0)),
        ],
        out_shape=[
            jax.ShapeDtypeStruct((b, t, d), F32),
            jax.ShapeDtypeStruct((b, t, d), BF16),
            jax.ShapeDtypeStruct((b, t, LANES), F32),
            jax.ShapeDtypeStruct((b, t, LANES), F32),
            jax.ShapeDtypeStruct((b * nt, SUBLANES, LANES), jnp.int32),
        ],
        compiler_params=_params("parallel", "parallel"),
        name="pool_route",
    )(x, x, x, mod, g, w_in, w_grp, pscale, w_out, mod_ffn, g_ffn, wr, br)


def _group_rows(tm):
    return TOP_K * tm + N_EXPERTS * BF16_ROWS


def _route(x, mod_ref, g_ref, wr_ref, br_ref, h_ref, w_ref, lp_ref, seg_ref):
    tm, d = x.shape
    h = _mod_rmsnorm(x, g_ref[...], mod_ref[0, :, 0:d], mod_ref[0, :, d:2 * d]).astype(BF16)
    h_ref[0] = h
    logits = jnp.dot(h, wr_ref[...], preferred_element_type=F32) + br_ref[...]
    lane = lax.broadcasted_iota(jnp.int32, logits.shape, 1).astype(F32)
    v1 = jnp.max(logits, axis=-1, keepdims=True)
    i1 = jnp.min(jnp.where(logits == v1, lane, float(LANES)), axis=-1, keepdims=True)
    rest = jnp.where(lane == i1, -jnp.inf, logits)
    v2 = jnp.max(rest, axis=-1, keepdims=True)
    i2 = jnp.min(jnp.where(rest == v2, lane, float(LANES)), axis=-1, keepdims=True)
    e2 = jnp.exp(v2 - v1)
    w1 = 1.0 / (1.0 + e2)
    w2 = e2 / (1.0 + e2)
    sel = jnp.where(lane == i1, 1.0, jnp.where(lane == i2, 1.0, 0.0))
    row = lax.broadcasted_iota(jnp.int32, (tm, tm), 0)
    col = lax.broadcasted_iota(jnp.int32, (tm, tm), 1)
    before = jnp.where(col < row, 1.0, 0.0).astype(BF16)
    rank = jnp.dot(before, sel.astype(BF16), preferred_element_type=F32)
    count = jnp.sum(sel, axis=0, keepdims=True)
    tiles = jnp.floor((count + (BF16_ROWS - 1)) * (1.0 / BF16_ROWS))
    er = lax.broadcasted_iota(jnp.int32, (LANES, LANES), 0)
    ec = lax.broadcasted_iota(jnp.int32, (LANES, LANES), 1)
    lower = jnp.where(er < ec, 1.0, 0.0).astype(BF16)
    tiles8 = jnp.broadcast_to(tiles, (SUBLANES, LANES)).astype(BF16)
    start = jnp.dot(tiles8, lower, preferred_element_type=F32)[0:1] * BF16_ROWS
    slot = start + rank
    lp1 = jnp.sum(jnp.where(lane == i1, slot, 0.0), axis=-1, keepdims=True)
    lp2 = jnp.sum(jnp.where(lane == i2, slot, 0.0), axis=-1, keepdims=True)
    w_ref[0] = jnp.where(lane == 0, w1, jnp.where(lane == 1, w2, 0.0))
    lp_ref[0] = jnp.where(lane == 0, lp1, jnp.where(lane == 1, lp2, 0.0))
    sub = lax.broadcasted_iota(jnp.int32, (SUBLANES, LANES), 0)
    seg = jnp.where(sub == 0, tiles * BF16_ROWS, jnp.where(sub == 1, start, 0.0))
    seg_ref[0] = seg.astype(jnp.int32)


def _segment_copies(tile, dst_ref, n_ref, ls_ref, make_copy, act):
    for e in range(N_EXPERTS):
        n = n_ref[tile * N_EXPERTS + e]
        ls = ls_ref[tile * N_EXPERTS + e]
        dst = dst_ref[tile * N_EXPERTS + e]
        size = BF16_ROWS
        while size <= TOKEN_TILE:
            done = n & (-2 * size)

            @pl.when((n & size) != 0)
            def _(size=size, done=done, ls=ls, dst=dst):
                act(make_copy(pl.multiple_of(ls + done, BF16_ROWS), pl.multiple_of(dst + done, BF16_ROWS), size))

            size *= 2


def _dispatch_kernel(dst_ref, n_ref, ls_ref, h_ref, lp_ref, xs_ref, grp_s, zero_s, sem, seg_sem):
    tile = pl.program_id(0)
    n_tiles = pl.num_programs(0)
    tm = h_ref.shape[0]

    @pl.when(tile == 0)
    def _():
        zero_s[...] = jnp.zeros_like(zero_s)

        def make_fill(local, glob, size):
            del local
            return pltpu.make_async_copy(zero_s.at[pl.ds(0, size), :], xs_ref.at[pl.ds(glob, size), :], sem)

        _segment_copies(n_tiles, dst_ref, n_ref, ls_ref, make_fill, lambda c: c.start())
        _segment_copies(n_tiles, dst_ref, n_ref, ls_ref, make_fill, lambda c: c.wait())

        first = dst_ref[(n_tiles + 1) * N_EXPERTS]
        count = n_ref[(n_tiles + 1) * N_EXPERTS]

        def fill_tile(k):
            return make_fill(0, pl.multiple_of(first + k * tm, tm), tm)

        lax.fori_loop(0, count, lambda k, c: (fill_tile(k).start(), c)[1], 0)
        lax.fori_loop(0, count, lambda k, c: (fill_tile(k).wait(), c)[1], 0)

    gb = grp_s.shape[1]
    buf = tile % 2
    lpt = lp_ref[...].T
    slot = lax.broadcasted_iota(jnp.int32, (gb, tm), 0).astype(F32)
    onehot = jnp.where(slot == lpt[0:1, :], 1.0, jnp.where(slot == lpt[1:2, :], 1.0, 0.0)).astype(BF16)
    grp_s[buf] = jnp.dot(onehot, h_ref[...], preferred_element_type=F32).astype(BF16)

    def copies(t, b, act):
        def make_copy(local, glob, size):
            return pltpu.make_async_copy(grp_s.at[b, pl.ds(local, size), :], xs_ref.at[pl.ds(glob, size), :], seg_sem.at[b])

        _segment_copies(t, dst_ref, n_ref, ls_ref, make_copy, act)

    copies(tile, buf, lambda c: c.start())

    @pl.when(tile > 0)
    def _():
        copies(tile - 1, 1 - buf, lambda c: c.wait())

    @pl.when(tile == n_tiles - 1)
    def _():
        copies(tile, buf, lambda c: c.wait())


def _dispatch(h, lp, dst, seg_n, seg_ls, slots, tm):
    n, d = h.shape
    assert tm == TOKEN_TILE
    grid_spec = pltpu.PrefetchScalarGridSpec(
        num_scalar_prefetch=3,
        grid=(n // tm,),
        in_specs=[
            pl.BlockSpec((tm, d), lambda i, *_: (i, 0)),
            pl.BlockSpec((tm, LANES), lambda i, *_: (i, 0)),
        ],
        out_specs=pl.BlockSpec(memory_space=pl.ANY),
        scratch_shapes=[pltpu.VMEM((2, _group_rows(tm), d), BF16), pltpu.VMEM((tm, d), BF16),
                        pltpu.SemaphoreType.DMA(()), pltpu.SemaphoreType.DMA((2,))],
    )
    return pl.pallas_call(
        _dispatch_kernel,
        grid_spec=grid_spec,
        out_shape=jax.ShapeDtypeStruct((slots, d), BF16),
        compiler_params=_params("arbitrary"),
        name="moe_dispatch",
    )(dst, seg_n, seg_ls, h, lp)


def _swiglu_step(h, wg, wu, wd):
    tf = wd.shape[0]
    au = jnp.dot(h, jnp.concatenate([wg, wu], axis=1), preferred_element_type=F32)
    return jnp.dot((_silu(au[:, :tf]) * au[:, tf:]).astype(BF16), wd, preferred_element_type=F32)


def _experts_kernel(te_ref, tr_ref, ts_ref, xs_ref, wg_ref, wu_ref, wd_ref, ys_ref, acc_s):
    del te_ref, ts_ref
    i = pl.program_id(0)
    f = pl.program_id(1)
    rows = tr_ref[i]
    part = xs_ref.shape[0] // EXPERT_TILE_PARTS

    @pl.when(f == 0)
    def _():
        acc_s[...] = jnp.zeros_like(acc_s)

    for k in range(1, EXPERT_TILE_PARTS + 1):
        @pl.when(jnp.logical_and(rows > (k - 1) * part, rows <= k * part))
        def _(used=k * part):
            acc_s[0:used, :] += _swiglu_step(xs_ref[0:used, :], wg_ref[0], wu_ref[0], wd_ref[0])

    @pl.when(f == pl.num_programs(1) - 1)
    def _():
        ys_ref[...] = acc_s[...].astype(BF16)


def _experts(xs, tile_expert, tile_rows, tile_src, wg, wu, wd, tm, tf):
    slots, d = xs.shape
    nf = wd.shape[1] // tf
    fidx = lambda f, tr, i: jnp.where(tr[i] > 0, f, nf - 1)
    grid_spec = pltpu.PrefetchScalarGridSpec(
        num_scalar_prefetch=3,
        grid=(slots // tm, nf),
        in_specs=[
            pl.BlockSpec((tm, d), lambda i, f, te, tv, ts: (ts[i], 0)),
            pl.BlockSpec((1, d, tf), lambda i, f, te, tv, ts: (te[i], 0, fidx(f, tv, i))),
            pl.BlockSpec((1, d, tf), lambda i, f, te, tv, ts: (te[i], 0, fidx(f, tv, i))),
            pl.BlockSpec((1, tf, d), lambda i, f, te, tv, ts: (te[i], fidx(f, tv, i), 0)),
        ],
        out_specs=pl.BlockSpec((tm, d), lambda i, f, te, tv, ts: (i, 0)),
        scratch_shapes=[pltpu.VMEM((tm, d), F32)],
    )
    return pl.pallas_call(
        _experts_kernel,
        grid_spec=grid_spec,
        out_shape=jax.ShapeDtypeStruct((slots, d), BF16),
        compiler_params=_params("parallel", "arbitrary"),
        name="moe_experts",
    )(tile_expert, tile_rows, tile_src, xs, wg, wu, wd)


def _combine_kernel(dst_ref, n_ref, ls_ref, x_ref, w_ref, lp_ref, mod_ref, gf_ref, ys_ref, out_ref, y_s, sem):
    d = x_ref.shape[-1]
    tm = x_ref.shape[1]
    gb = y_s.shape[1]
    tile = pl.program_id(0) * pl.num_programs(1) + pl.program_id(1)
    n_tiles = pl.num_programs(0) * pl.num_programs(1)
    buf = tile % 2

    def fetch(t, b, act):
        def make_copy(local, glob, size):
            return pltpu.make_async_copy(ys_ref.at[pl.ds(glob, size), :], y_s.at[b, pl.ds(local, size), :], sem.at[b])

        _segment_copies(t, dst_ref, n_ref, ls_ref, make_copy, act)

    def start_fetch(t, b):
        y_s[b, TOP_K * tm:, :] = jnp.zeros((gb - TOP_K * tm, d), BF16)
        fetch(t, b, lambda c: c.start())

    @pl.when(tile == 0)
    def _():
        start_fetch(tile, buf)

    @pl.when(tile + 1 < n_tiles)
    def _():
        start_fetch(tile + 1, 1 - buf)

    lp = lp_ref[0]
    w = w_ref[0]
    slot = lax.broadcasted_iota(jnp.int32, (tm, gb), 1).astype(F32)
    pick1 = jnp.where(slot == lp[:, 0:1], 1.0, 0.0).astype(BF16)
    pick2 = jnp.where(slot == lp[:, 1:2], 1.0, 0.0).astype(BF16)

    fetch(tile, buf, lambda c: c.wait())

    y = y_s[buf]
    y1 = jnp.dot(pick1, y, preferred_element_type=F32)
    y2 = jnp.dot(pick2, y, preferred_element_type=F32)
    f = w[:, 0:1] * y1 + w[:, 1:2] * y2
    x = x_ref[0] + mod_ref[0, :, 2 * d:3 * d] * f
    var = jnp.mean(x * x, axis=-1, keepdims=True)
    out_ref[0] = (x * lax.rsqrt(var + RMS_EPS)) * gf_ref[...]


def _combine(x, w, lp, mod, gf, ys, dst, seg_n, seg_ls, tm):
    b, t, d = x.shape
    assert tm == TOKEN_TILE
    tok = lambda i, j, *_: (i, j, 0)
    grid_spec = pltpu.PrefetchScalarGridSpec(
        num_scalar_prefetch=3,
        grid=(b, t // tm),
        in_specs=[
            pl.BlockSpec((1, tm, d), tok),
            pl.BlockSpec((1, tm, LANES), tok),
            pl.BlockSpec((1, tm, LANES), tok),
            pl.BlockSpec((1, 1, 3 * d), lambda i, j, *_: (i, 0, 0)),
            pl.BlockSpec((1, d), lambda i, j, *_: (0, 0)),
            pl.BlockSpec(memory_space=pl.ANY),
        ],
        out_specs=pl.BlockSpec((1, tm, d), tok),
        scratch_shapes=[pltpu.VMEM((2, _group_rows(tm), d), BF16), pltpu.SemaphoreType.DMA((2,))],
    )
    return pl.pallas_call(
        _combine_kernel,
        grid_spec=grid_spec,
        out_shape=jax.ShapeDtypeStruct((b, t, d), F32),
        compiler_params=_params("arbitrary", "arbitrary"),
        name="moe_combine",
    )(dst, seg_n, seg_ls, x, w, lp, mod, gf, ys)


def _trunk(x, mods_mix, mods_ffn, p):
    b, t, d = x.shape
    tm = TOKEN_TILE
    assert t % tm == 0

    q, k, vt = _qkv(x, mods_mix[0], p["ln_mix_g"][0], p["wq"], p["wk"], p["wvt"], _largest_tile(t, QKV_TILE))
    rows = t // GRID_W
    rpp = 16 if (rows // 2) % 16 == 0 else 1
    o = _na(q, k, vt, p["na_bias"], head_groups=2, row_pairs_per_step=rpp)
    ff = p["ffn_wd"].shape[0]
    x = _proj_ffn(x, o, mods_mix[0], mods_ffn[0], p["ln_ffn_g"][0], p["wo"], p["ffn_wg"], p["ffn_wu"], p["ffn_wd"],
                  tm, _largest_tile(ff, FF_TILE_DENSE))

    x, h, w, lp, seg = _pool_route(x, mods_mix[1], p["ln_mix_g"][1], p["pool_w_in"], p["pool_w_grp"], p["pool_scale"],
                                   p["pool_w_out"], mods_ffn[1], p["ln_ffn_g"][1], p["wr"], p["br"], tm)

    n = b * t
    nt = n // tm
    seg_n = seg[:, 0, :N_EXPERTS]
    seg_ls = seg[:, 1, :N_EXPERTS]
    before = jnp.cumsum(seg_n, axis=0) - seg_n
    total = jnp.sum(seg_n, axis=0)
    te = EXPERT_TILE
    group = ((total + te - 1) // te) * te
    ends = jnp.cumsum(group)
    dst = (ends - group)[None, :] + before
    n_tiles = -(-(TOP_K * n + nt * N_EXPERTS * (BF16_ROWS - 1)) // te) + N_EXPERTS
    tile_start = jnp.arange(n_tiles, dtype=jnp.int32) * te
    tile_live = (tile_start < ends[-1]).astype(jnp.int32)
    tile_expert = jnp.sum((ends[None, :] <= tile_start[:, None]).astype(jnp.int32), axis=1)
    last_expert = jnp.sum((ends <= ends[-1] - 1).astype(jnp.int32))
    tile_expert = jnp.minimum(jnp.where(tile_live > 0, tile_expert, last_expert), N_EXPERTS - 1)
    tile_src = jnp.minimum(jnp.arange(n_tiles, dtype=jnp.int32), jnp.sum(tile_live) - 1)
    filled_end = (ends - group + total)[tile_expert]
    tile_rows = jnp.where(tile_live > 0, jnp.clip(filled_end - tile_start, 0, te), 0).astype(jnp.int32)
    lead = jnp.arange(N_EXPERTS) == 0
    fills = (n_tiles * te - ends[-1]) // tm
    dst = jnp.concatenate([dst, (ends - group + total)[None, :], jnp.where(lead, ends[-1], 0)[None, :]], axis=0)
    seg_n = jnp.concatenate([seg_n, (group - total)[None, :], jnp.where(lead, fills, 0)[None, :]], axis=0)
    seg_ls = jnp.concatenate([seg_ls, jnp.zeros((2, N_EXPERTS), seg_ls.dtype)], axis=0)
    dst, seg_n, seg_ls = (a.reshape(-1).astype(jnp.int32) for a in (dst, seg_n, seg_ls))

    xs = _dispatch(h.reshape(n, d), lp.reshape(n, LANES), dst, seg_n, seg_ls, n_tiles * te, tm)
    ys = _experts(xs, tile_expert, tile_rows, tile_src, p["moe_wg"], p["moe_wu"], p["moe_wd"], te,
                  _largest_tile(ff, FF_TILE_MOE))
    return _combine(x, w, lp, mods_ffn[1], p["ln_f_g"], ys, dst, seg_n, seg_ls, tm)


def kernel(x_prompt, x_sample, c_prompt, c_sample, ln_mix_g, ada_mix_w, ada_mix_b, ln_ffn_g, ada_ffn_w, ada_ffn_b, na_w_qkv, na_rpb, na_w_o, pool_w_in, pool_w_grp, pool_scale, pool_w_out, ffn_w_gate, ffn_w_up, ffn_w_down, moe_w_router, moe_b_router, moe_w_gate, moe_w_up, moe_w_down, ln_f_g):
    d = x_prompt.shape[-1]
    depth = ln_mix_g.shape[0]
    assert depth == 2 and d == NA_HEADS * HEAD_DIM
    bp, bs = c_prompt.shape[0], c_sample.shape[0]

    pad = (-(bp + bs)) % SUBLANES
    c_all = jnp.concatenate([c_prompt, c_sample, jnp.zeros((pad, d), F32)], axis=0)
    mods_mix = _adaln(c_all, ada_mix_w, ada_mix_b)
    mods_ffn = _adaln(c_all, ada_ffn_w, ada_ffn_b)

    wqkv = na_w_qkv[0]
    ne = moe_w_router.shape[-1]
    assert ne == N_EXPERTS
    p = {
        "ln_mix_g": ln_mix_g.reshape(depth, 1, d),
        "ln_ffn_g": ln_ffn_g.reshape(depth, 1, d),
        "ln_f_g": ln_f_g.reshape(1, d),
        "wq": wqkv[:, 0:d].astype(BF16),
        "wk": wqkv[:, d:2 * d].astype(BF16),
        "wvt": wqkv[:, 2 * d:3 * d].T.astype(BF16),
        "na_bias": _na_bias_table(na_rpb[0]),
        "wo": na_w_o[0].astype(BF16),
        "ffn_wg": ffn_w_gate[0].astype(BF16),
        "ffn_wu": ffn_w_up[0].astype(BF16),
        "ffn_wd": ffn_w_down[0].astype(BF16),
        "pool_w_in": pool_w_in[0].astype(BF16),
        "pool_w_grp": pool_w_grp[0].astype(BF16),
        "pool_scale": pool_scale[0].reshape(1, d),
        "pool_w_out": pool_w_out[0].astype(BF16),
        "wr": jnp.pad(moe_w_router[0], ((0, 0), (0, LANES - ne))).astype(BF16),
        "br": jnp.pad(moe_b_router[0].astype(F32), (0, LANES - ne), constant_values=MASK_BIAS).reshape(1, LANES),
        "moe_wg": moe_w_gate[0].astype(BF16),
        "moe_wu": moe_w_up[0].astype(BF16),
        "moe_wd": moe_w_down[0].astype(BF16),
    }

    def group_mods(m, lo, n):
        return m[:, lo:lo + n, None, :]

    y_prompt = _trunk(x_prompt, group_mods(mods_mix, 0, bp), group_mods(mods_ffn, 0, bp), p)
    y_sample = _trunk(x_sample, group_mods(mods_mix, bp, bs), group_mods(mods_ffn, bp, bs), p)
    return (y_prompt, y_sample)
```
